```python
import math
import jax
import jax.numpy as jnp
from jax import lax
import numpy as np

D_MODEL = 1024
BATCH = 2
SEQ = 8192
DEPTH = 4
DEC_BATCH = 32
DEC_SEQ = 1
PAST_LEN = 8192
PAGE_SIZE = 128

SSM_GROUP = 16
SSM_GROUPS = D_MODEL // SSM_GROUP
SSM_STATE = 64
DT_MIN = 1e-3
DT_MAX = 1e-1
N_HEADS = 16
HEAD_DIM = D_MODEL // N_HEADS
MOBA_BLOCK = 256
MOBA_TOPK = 3
Q_CHUNK = 32
D_FF = 2816
N_EXPERTS = 8
TOP_K = 2
D_FF_EXPERT = 1408
N_SSM_LAYERS = (DEPTH + 1) // 2
N_ATTN_LAYERS = DEPTH // 2
N_DENSE_FFN = (DEPTH + 1) // 2
N_MOE_FFN = DEPTH // 2
RMS_EPS = 1e-6

kernel_name = 'hybrid_s5_moba_adaln_decoder_step'


def _rmsnorm(x, g):
    xf = x.astype(jnp.float32)
    y = xf * lax.rsqrt(jnp.mean(xf * xf, axis=-1, keepdims=True) + RMS_EPS)
    return (y * g.astype(jnp.float32)).astype(x.dtype)


def _adaln(c, w, b):
    mod = jax.nn.silu(c) @ w + b
    return jnp.split(mod[:, None, :], 6, axis=-1)


def _modulate(x, g, shift, scale):
    return _rmsnorm(x, g) * (1 + scale) + shift


def _swiglu(h, w_gate, w_up, w_down):
    return (jax.nn.silu(h @ w_gate) * (h @ w_up)) @ w_down


def _s5_discretise(a_re, a_im, log_dt, b_re, b_im):
    f32 = jnp.float32
    a_re = a_re.astype(f32)
    a_im = a_im.astype(f32)
    dt = jnp.exp(log_dt.astype(f32))[:, None]
    mag = jnp.exp(dt * a_re)
    ab_re = mag * jnp.cos(dt * a_im)
    ab_im = mag * jnp.sin(dt * a_im)
    den = a_re * a_re + a_im * a_im
    num_re = ab_re - 1.0
    coef_re = (num_re * a_re + ab_im * a_im) / den
    coef_im = (ab_im * a_re - num_re * a_im) / den
    b_re = b_re.astype(f32)
    b_im = b_im.astype(f32)
    bb_re = coef_re[..., None] * b_re - coef_im[..., None] * b_im
    bb_im = coef_re[..., None] * b_im + coef_im[..., None] * b_re
    return ab_re, ab_im, bb_re, bb_im


def _complex_affine_combine(e1, e2):
    a1r, a1i, b1r, b1i = e1
    a2r, a2i, b2r, b2i = e2
    return (a1r * a2r - a1i * a2i,
            a1r * a2i + a1i * a2r,
            a2r * b1r - a2i * b1i + b2r,
            a2r * b1i + a2i * b1r + b2i)


def _s5_mixer(u, h0_re, h0_im, a_re, a_im, log_dt, b_re, b_im, c_re, c_im, d_skip,
              w_glu_out, w_glu_gate):
    f32 = jnp.float32
    bn, t_len, _ = u.shape
    ab_re, ab_im, bb_re, bb_im = _s5_discretise(a_re, a_im, log_dt, b_re, b_im)
    ug = u.astype(f32).reshape(bn, t_len, SSM_GROUPS, SSM_GROUP)
    bu_re = jnp.einsum('btgj,gpj->btgp', ug, bb_re)
    bu_im = jnp.einsum('btgj,gpj->btgp', ug, bb_im)
    if h0_re is not None:
        h0r = h0_re.astype(f32)
        h0i = h0_im.astype(f32)
        bu_re = bu_re.at[:, 0].add(ab_re * h0r - ab_im * h0i)
        bu_im = bu_im.at[:, 0].add(ab_re * h0i + ab_im * h0r)
    a_r = jnp.broadcast_to(ab_re, bu_re.shape)
    a_i = jnp.broadcast_to(ab_im, bu_re.shape)
    _, _, s_re, s_im = lax.associative_scan(_complex_affine_combine,
                                            (a_r, a_i, bu_re, bu_im), axis=1)
    y = (jnp.einsum('btgp,gjp->btgj', s_re, c_re.astype(f32))
         - jnp.einsum('btgp,gjp->btgj', s_im, c_im.astype(f32)))
    y = y.reshape(bn, t_len, D_MODEL) + d_skip.astype(f32) * u.astype(f32)
    g = jax.nn.gelu(y).astype(u.dtype)
    out = (g @ w_glu_out) * jax.nn.sigmoid(g @ w_glu_gate)
    return out.astype(u.dtype), s_re[:, -1], s_im[:, -1]


def _alibi_slopes():
    return 2.0 ** (-8.0 * jnp.arange(1, N_HEADS + 1, dtype=jnp.float32) / N_HEADS)


def _qkv(h, w_qkv):
    bn, t_len, _ = h.shape
    q, k, v = jnp.split(h @ w_qkv, 3, axis=-1)
    shp = (bn, t_len, N_HEADS, HEAD_DIM)
    return q.reshape(shp), k.reshape(shp), v.reshape(shp)


def _kv_blocks(k, v):
    bn, l_len = k.shape[:2]
    nb = -(-l_len // MOBA_BLOCK)
    pad = ((0, 0), (0, nb * MOBA_BLOCK - l_len), (0, 0), (0, 0))
    shp = (bn, nb, MOBA_BLOCK, N_HEADS, HEAD_DIM)
    kb = jnp.pad(k, pad).reshape(shp)
    vb = jnp.pad(v, pad).reshape(shp)
    k_mean = jnp.mean(kb.astype(jnp.float32), axis=2)
    return kb, vb, k_mean


def _moba_attend(q, q_pos, kb, vb, k_mean):
    f32 = jnp.float32
    bn, nq = q.shape[:2]
    nb = kb.shape[1]
    qf = q.astype(f32)
    own = q_pos // MOBA_BLOCK
    gate = jnp.einsum('bqhd,bnhd->bhqn', qf, k_mean)
    past = jnp.arange(nb)[None, :] < own[:, None]
    gate = jnp.where(past[None, None], gate, -jnp.inf)
    if nb < MOBA_TOPK:
        gate = jnp.pad(gate, ((0, 0), (0, 0), (0, 0), (0, MOBA_TOPK - nb)),
                       constant_values=-jnp.inf)
    _, sel = lax.top_k(gate, MOBA_TOPK)
    sel_valid = jnp.arange(MOBA_TOPK)[None, :] < own[:, None]
    own_b = jnp.broadcast_to(own[None, None, :, None], (bn, N_HEADS, nq, 1))
    sel = jnp.where(sel_valid[None, None], sel, own_b)
    blocks = jnp.concatenate([sel, own_b], axis=-1)
    slot_valid = jnp.concatenate([sel_valid, jnp.ones((nq, 1), bool)], axis=-1)
    bi = jnp.arange(bn)[:, None, None, None]
    hi = jnp.arange(N_HEADS)[None, :, None, None]
    k_g = kb[bi, blocks, :, hi]
    v_g = vb[bi, blocks, :, hi]
    k_pos = blocks[..., None] * MOBA_BLOCK + jnp.arange(MOBA_BLOCK)
    dist = (q_pos[None, None, :, None, None] - k_pos).astype(f32)
    mask = slot_valid[None, None, :, :, None] & (dist >= 0)
    s = jnp.einsum('bqhd,bhqskd->bhqsk', qf, k_g.astype(f32)) * (HEAD_DIM ** -0.5)
    s = s - _alibi_slopes()[None, :, None, None, None] * dist
    s = jnp.where(mask, s, -jnp.inf)
    p = jax.nn.softmax(s.reshape(bn, N_HEADS, nq, -1), axis=-1).reshape(s.shape)
    out = jnp.einsum('bhqsk,bhqskd->bqhd', p, v_g.astype(f32))
    return out.astype(q.dtype)


def _moba_prompt(h, w_qkv, w_o):
    bn, t_len, _ = h.shape
    q, k, v = _qkv(h, w_qkv)
    kb, vb, k_mean = _kv_blocks(k, v)
    n_chunks = t_len // Q_CHUNK
    q_chunks = q.reshape(bn, n_chunks, Q_CHUNK, N_HEADS, HEAD_DIM).transpose(1, 0, 2, 3, 4)
    pos_chunks = jnp.arange(t_len, dtype=jnp.int32).reshape(n_chunks, Q_CHUNK)
    out = lax.map(lambda qp: _moba_attend(qp[0], qp[1], kb, vb, k_mean),
                  (q_chunks, pos_chunks))
    out = out.transpose(1, 0, 2, 3, 4).reshape(bn, t_len, D_MODEL)
    return out @ w_o, k, v


def _moba_sample(h, cache_k_l, cache_v_l, page_table, w_qkv, w_o):
    bn, s_len, _ = h.shape
    q, k, v = _qkv(h, w_qkv)
    past_len = page_table.shape[1] * cache_k_l.shape[1]
    rows = (bn, past_len, N_HEADS, HEAD_DIM)
    k_all = jnp.concatenate([cache_k_l[page_table].reshape(rows), k.astype(cache_k_l.dtype)], axis=1)
    v_all = jnp.concatenate([cache_v_l[page_table].reshape(rows), v.astype(cache_v_l.dtype)], axis=1)
    kb, vb, k_mean = _kv_blocks(k_all, v_all)
    q_pos = past_len + jnp.arange(s_len, dtype=jnp.int32)
    out = _moba_attend(q, q_pos, kb, vb, k_mean).reshape(bn, s_len, D_MODEL)
    return out @ w_o, k, v


def _moe_swiglu(h, w_router, b_router, w_gate, w_up, w_down):
    logits = (h @ w_router).astype(jnp.float32) + b_router.astype(jnp.float32)
    top_val, top_idx = lax.top_k(logits, TOP_K)
    weights = jax.nn.softmax(top_val, axis=-1)
    combine = jnp.sum(jax.nn.one_hot(top_idx, N_EXPERTS, dtype=jnp.float32)
                      * weights[..., None], axis=-2)
    out = jnp.zeros(h.shape, jnp.float32)
    for e in range(N_EXPERTS):
        y_e = _swiglu(h, w_gate[e], w_up[e], w_down[e]).astype(jnp.float32)
        out = out + combine[..., e:e + 1] * y_e
    return out.astype(h.dtype)


def _trunk(x, c, p, past):
    ssm_re, ssm_im, new_k, new_v = [], [], [], []
    for layer in range(DEPTH):
        i = layer // 2
        sh1, sc1, g1, sh2, sc2, g2 = _adaln(c, p['ada_w'][layer], p['ada_b'][layer])
        h = _modulate(x, p['norm_mix'][layer], sh1, sc1)
        if layer % 2 == 0:
            h0_re = None if past is None else past[3][i]
            h0_im = None if past is None else past[4][i]
            y, s_re, s_im = _s5_mixer(h, h0_re, h0_im, p['ssm_a_re'][i], p['ssm_a_im'][i],
                                      p['ssm_log_dt'][i], p['ssm_b_re'][i], p['ssm_b_im'][i],
                                      p['ssm_c_re'][i], p['ssm_c_im'][i], p['ssm_d'][i],
                                      p['ssm_w_glu_out'][i], p['ssm_w_glu_gate'][i])
            ssm_re.append(s_re)
            ssm_im.append(s_im)
        else:
            if past is None:
                y, k, v = _moba_prompt(h, p['attn_w_qkv'][i], p['attn_w_o'][i])
            else:
                y, k, v = _moba_sample(h, past[0][i], past[1][i], past[2],
                                       p['attn_w_qkv'][i], p['attn_w_o'][i])
            new_k.append(k)
            new_v.append(v)
        x = x + g1 * y
        h = _modulate(x, p['norm_ffn'][layer], sh2, sc2)
        if layer % 2 == 0:
            f = _swiglu(h, p['ffn_w_gate'][i], p['ffn_w_up'][i], p['ffn_w_down'][i])
        else:
            f = _moe_swiglu(h, p['moe_w_router'][i], p['moe_b_router'][i],
                            p['moe_w_gate'][i], p['moe_w_up'][i], p['moe_w_down'][i])
        x = x + g2 * f
    y_out = _rmsnorm(x, p['norm_final'])
    return y_out, jnp.stack(ssm_re), jnp.stack(ssm_im), jnp.stack(new_k), jnp.stack(new_v)


def setup_inputs(seed: int = 0) -> dict:
    key = jax.random.key(seed)
    keys = iter(jax.random.split(key, 48))
    f32 = jnp.float32

    def normal(shape, scale):
        return jax.random.normal(next(keys), shape, f32) * scale

    d = D_MODEL
    n_pages = PAST_LEN // PAGE_SIZE
    n_used = DEC_BATCH * n_pages
    n_phys = n_used + max(1, n_used // 4)
    page_table = jax.random.permutation(next(keys), n_phys)[:n_used]
    page_table = page_table.reshape(DEC_BATCH, n_pages).astype(jnp.int32)
    kv_shape = (N_ATTN_LAYERS, n_phys, PAGE_SIZE, N_HEADS, HEAD_DIM)
    ssm_state_shape = (N_SSM_LAYERS, DEC_BATCH, SSM_GROUPS, SSM_STATE)
    a_shape = (N_SSM_LAYERS, SSM_GROUPS, SSM_STATE)
    return {
        'x_prompt': normal((BATCH, SEQ, d), 1.0),
        'x_sample': normal((DEC_BATCH, DEC_SEQ, d), 1.0),
        'cache_k': normal(kv_shape, 1.0),
        'cache_v': normal(kv_shape, 1.0),
        'page_table': page_table,
        'state_ssm_re': normal(ssm_state_shape, 0.1),
        'state_ssm_im': normal(ssm_state_shape, 0.1),
        'c_prompt': normal((BATCH, d), 1.0),
        'c_sample': normal((DEC_BATCH, d), 1.0),
        'ada_w': normal((DEPTH, d, 6 * d), 0.5 * d ** -0.5),
        'ada_b': normal((DEPTH, 6 * d), 0.02),
        'norm_mix': 1.0 + normal((DEPTH, d), 0.05),
        'norm_ffn': 1.0 + normal((DEPTH, d), 0.05),
        'norm_final': 1.0 + normal((d,), 0.05),
        'ssm_a_re': -0.5 + normal(a_shape, 0.01),
        'ssm_a_im': jnp.pi * jnp.arange(SSM_STATE, dtype=f32) + normal(a_shape, 0.01),
        'ssm_log_dt': jax.random.uniform(next(keys), (N_SSM_LAYERS, SSM_GROUPS), f32,
                                         math.log(DT_MIN), math.log(DT_MAX)),
        'ssm_b_re': normal((N_SSM_LAYERS, SSM_GROUPS, SSM_STATE, SSM_GROUP), SSM_GROUP ** -0.5),
        'ssm_b_im': normal((N_SSM_LAYERS, SSM_GROUPS, SSM_STATE, SSM_GROUP), SSM_GROUP ** -0.5),
        'ssm_c_re': normal((N_SSM_LAYERS, SSM_GROUPS, SSM_GROUP, SSM_STATE), SSM_STATE ** -0.5),
        'ssm_c_im': normal((N_SSM_LAYERS, SSM_GROUPS, SSM_GROUP, SSM_STATE), SSM_STATE ** -0.5),
        'ssm_d': normal((N_SSM_LAYERS, d), 1.0),
        'ssm_w_glu_out': normal((N_SSM_LAYERS, d, d), d ** -0.5),
        'ssm_w_glu_gate': normal((N_SSM_LAYERS, d, d), d ** -0.5),
        'attn_w_qkv': normal((N_ATTN_LAYERS, d, 3 * d), d ** -0.5),
        'attn_w_o': normal((N_ATTN_LAYERS, d, d), d ** -0.5),
        'ffn_w_gate': normal((N_DENSE_FFN, d, D_FF), d ** -0.5),
        'ffn_w_up': normal((N_DENSE_FFN, d, D_FF), d ** -0.5),
        'ffn_w_down': normal((N_DENSE_FFN, D_FF, d), D_FF ** -0.5),
        'moe_w_router': normal((N_MOE_FFN, d, N_EXPERTS), d ** -0.5),
        'moe_b_router': normal((N_MOE_FFN, N_EXPERTS), 0.01),
        'moe_w_gate': normal((N_MOE_FFN, N_EXPERTS, d, D_FF_EXPERT), d ** -0.5),
        'moe_w_up': normal((N_MOE_FFN, N_EXPERTS, d, D_FF_EXPERT), d ** -0.5),
        'moe_w_down': normal((N_MOE_FFN, N_EXPERTS, D_FF_EXPERT, d), D_FF_EXPERT ** -0.5),
    }


def reference(x_prompt, x_sample, cache_k, cache_v, page_table, state_ssm_re, state_ssm_im,
              c_prompt, c_sample, ada_w, ada_b, norm_mix, norm_ffn, norm_final,
              ssm_a_re, ssm_a_im, ssm_log_dt, ssm_b_re, ssm_b_im, ssm_c_re, ssm_c_im, ssm_d,
              ssm_w_glu_out, ssm_w_glu_gate, attn_w_qkv, attn_w_o,
              ffn_w_gate, ffn_w_up, ffn_w_down,
              moe_w_router, moe_b_router, moe_w_gate, moe_w_up, moe_w_down):
    params = {
        'ada_w': ada_w, 'ada_b': ada_b, 'norm_mix': norm_mix, 'norm_ffn': norm_ffn,
        'norm_final': norm_final,
        'ssm_a_re': ssm_a_re, 'ssm_a_im': ssm_a_im, 'ssm_log_dt': ssm_log_dt,
        'ssm_b_re': ssm_b_re, 'ssm_b_im': ssm_b_im, 'ssm_c_re': ssm_c_re, 'ssm_c_im': ssm_c_im,
        'ssm_d': ssm_d, 'ssm_w_glu_out': ssm_w_glu_out, 'ssm_w_glu_gate': ssm_w_glu_gate,
        'attn_w_qkv': attn_w_qkv, 'attn_w_o': attn_w_o,
        'ffn_w_gate': ffn_w_gate, 'ffn_w_up': ffn_w_up, 'ffn_w_down': ffn_w_down,
        'moe_w_router': moe_w_router, 'moe_b_router': moe_b_router,
        'moe_w_gate': moe_w_gate, 'moe_w_up': moe_w_up, 'moe_w_down': moe_w_down,
    }
    y_prompt, ssm_re_p, ssm_im_p, k_p, v_p = _trunk(x_prompt, c_prompt, params, None)
    y_sample, ssm_re_s, ssm_im_s, k_s, v_s = _trunk(
        x_sample, c_sample, params, (cache_k, cache_v, page_table, state_ssm_re, state_ssm_im))
    return (y_prompt, y_sample, ssm_re_p, ssm_im_p, ssm_re_s, ssm_im_s, k_p, v_p, k_s, v_s)
```

```python
import functools

import jax
import jax.numpy as jnp
from jax import lax
from jax.experimental import pallas as pl
from jax.experimental.pallas import tpu as pltpu

F32 = jnp.float32
BF16 = jnp.bfloat16
HI = lax.Precision.HIGHEST
NEG_INF = float("-inf")

SUBLANES = 8
LANES = 128

RMS_EPS = 1e-6
SSM_GROUP = 16
SSM_STATE = 64
PACK_GROUPS = LANES // SSM_GROUP
PACK_STATES = PACK_GROUPS * SSM_STATE
N_HEADS = 16
HEAD_DIM = 64
MOBA_BLOCK = 256
MOBA_TOPK = 3
TOP_K = 2

ROW_TILE = 512
S5_TILE = 256
VMEM_LIMIT = 56 * 1024 * 1024


def _cparams(sem):
    return pltpu.CompilerParams(dimension_semantics=sem, vmem_limit_bytes=VMEM_LIMIT)


def _wdot(a, w):
    if w.dtype == F32:
        return jnp.dot(a.astype(F32), w, precision=HI, preferred_element_type=F32)
    return jnp.dot(a.astype(BF16), w, preferred_element_type=F32)


def _row_tile(t):
    return ROW_TILE if t % ROW_TILE == 0 else t


def _mod_spec(arr, tm):
    d = arr.shape[-1]
    if arr.shape[1] == 1:
        return pl.BlockSpec((1, 1, d), lambda b, t: (b, 0, 0))
    return pl.BlockSpec((1, tm, d), lambda b, t: (b, t, 0))


def _adaln_kernel(c_ref, w_ref, b_ref, o_ref):
    c = c_ref[...]
    s = c * jax.nn.sigmoid(c)
    o_ref[0] = jnp.dot(s, w_ref[0], precision=HI, preferred_element_type=F32) + b_ref[0]


def _adaln(c_all, ada_w, ada_b):
    depth, d, n = ada_w.shape
    r = c_all.shape[0]
    tn = 1536
    return pl.pallas_call(
        _adaln_kernel,
        grid=(depth, n // tn),
        in_specs=[pl.BlockSpec((r, d), lambda l, j: (0, 0)),
                  pl.BlockSpec((1, d, tn), lambda l, j: (l, 0, j)),
                  pl.BlockSpec((1, 1, tn), lambda l, j: (l, 0, j))],
        out_specs=pl.BlockSpec((1, r, tn), lambda l, j: (l, 0, j)),
        out_shape=jax.ShapeDtypeStruct((depth, r, n), F32),
        compiler_params=_cparams(("arbitrary", "arbitrary")),
        name="adaln",
    )(c_all, ada_w, ada_b.reshape(depth, 1, n))


def _rms(x, g):
    return x * lax.rsqrt(jnp.mean(x * x, axis=-1, keepdims=True) + RMS_EPS) * g


def _modulate_kernel(x_ref, g_ref, sh_ref, sc_ref, o_ref):
    h = _rms(x_ref[0], g_ref[...]) * (1.0 + sc_ref[0]) + sh_ref[0]
    o_ref[0] = h.astype(o_ref.dtype)


def _modulate_router_kernel(x_ref, g_ref, sh_ref, sc_ref, wr_ref, br_ref, o_ref, comb_ref):
    h = _rms(x_ref[0], g_ref[...]) * (1.0 + sc_ref[0]) + sh_ref[0]
    o_ref[0] = h.astype(o_ref.dtype)
    logits = jnp.dot(h, wr_ref[...], precision=HI, preferred_element_type=F32) + br_ref[...]
    lane = lax.broadcasted_iota(jnp.int32, logits.shape, 1)
    m1 = jnp.max(logits, axis=-1, keepdims=True)
    i1 = jnp.min(jnp.where(logits == m1, lane, LANES), axis=-1, keepdims=True)
    rest = jnp.where(lane == i1, NEG_INF, logits)
    m2 = jnp.max(rest, axis=-1, keepdims=True)
    i2 = jnp.min(jnp.where(rest == m2, lane, LANES), axis=-1, keepdims=True)
    e2 = jnp.exp(m2 - m1)
    den = 1.0 + e2
    comb_ref[0] = jnp.where(lane == i1, 1.0 / den, 0.0) + jnp.where(lane == i2, e2 / den, 0.0)


def _modulate(x, g, shift, scale, out_dtype, router=None):
    bn, t, d = x.shape
    tm = _row_tile(t)
    grid = (bn, t // tm)
    x_spec = pl.BlockSpec((1, tm, d), lambda b, i: (b, i, 0))
    in_specs = [x_spec, pl.BlockSpec((1, d), lambda b, i: (0, 0)), _mod_spec(shift, tm), _mod_spec(scale, tm)]
    args = [x, g.reshape(1, d), shift, scale]
    if router is None:
        return pl.pallas_call(
            _modulate_kernel, grid=grid, in_specs=in_specs, out_specs=x_spec,
            out_shape=jax.ShapeDtypeStruct((bn, t, d), out_dtype),
            compiler_params=_cparams(("arbitrary", "arbitrary")), name="modulate",
        )(*args)
    w_router, b_router = router
    n_exp = w_router.shape[1]
    wr = jnp.pad(w_router, ((0, 0), (0, LANES - n_exp)))
    br = jnp.pad(b_router.reshape(1, n_exp), ((0, 0), (0, LANES - n_exp)), constant_values=NEG_INF)
    in_specs += [pl.BlockSpec((d, LANES), lambda b, i: (0, 0)), pl.BlockSpec((1, LANES), lambda b, i: (0, 0))]
    return pl.pallas_call(
        _modulate_router_kernel, grid=grid, in_specs=in_specs,
        out_specs=[x_spec, pl.BlockSpec((1, tm, LANES), lambda b, i: (b, i, 0))],
        out_shape=[jax.ShapeDtypeStruct((bn, t, d), out_dtype), jax.ShapeDtypeStruct((bn, t, LANES), F32)],
        compiler_params=_cparams(("arbitrary", "arbitrary")), name="modulate_router",
    )(*args, wr, br)


def _final_norm_kernel(x_ref, g_ref, o_ref):
    o_ref[0] = _rms(x_ref[0], g_ref[...])


def _final_norm(x, g):
    bn, t, d = x.shape
    tm = _row_tile(t)
    x_spec = pl.BlockSpec((1, tm, d), lambda b, i: (b, i, 0))
    return pl.pallas_call(
        _final_norm_kernel, grid=(bn, t // tm),
        in_specs=[x_spec, pl.BlockSpec((1, d), lambda b, i: (0, 0))], out_specs=x_spec,
        out_shape=jax.ShapeDtypeStruct((bn, t, d), F32),
        compiler_params=_cparams(("arbitrary", "arbitrary")), name="final_norm",
    )(x, g.reshape(1, d))


def _proj_residual_kernel(x_ref, a_ref, w_ref, gate_ref, o_ref):
    y = _wdot(a_ref[0], w_ref[...])
    o_ref[0] = x_ref[0] + gate_ref[0] * y


def _glu_residual_kernel(x_ref, a_ref, wo_ref, wg_ref, gate_ref, o_ref):
    a = a_ref[0]
    y = _wdot(a, wo_ref[...])
    z = _wdot(a, wg_ref[...])
    o_ref[0] = x_ref[0] + gate_ref[0] * (y * jax.nn.sigmoid(z))


def _proj_residual(x, a, weights, gate):
    bn, t, d = x.shape
    tm = _row_tile(t)
    x_spec = pl.BlockSpec((1, tm, d), lambda b, i: (b, i, 0))
    w_spec = pl.BlockSpec((d, d), lambda b, i: (0, 0))
    kern = _proj_residual_kernel if len(weights) == 1 else _glu_residual_kernel
    return pl.pallas_call(
        kern, grid=(bn, t // tm),
        in_specs=[x_spec, x_spec] + [w_spec] * len(weights) + [_mod_spec(gate, tm)],
        out_specs=x_spec, out_shape=jax.ShapeDtypeStruct((bn, t, d), F32),
        compiler_params=_cparams(("arbitrary", "arbitrary")), name="proj_residual",
    )(x, a, *weights, gate)


def _swiglu_kernel(x_ref, h_ref, wg_ref, wu_ref, wd_ref, gate_ref, o_ref, acc_ref):
    e, f = pl.program_id(2), pl.program_id(3)

    @pl.when((e == 0) & (f == 0))
    def _():
        acc_ref[...] = jnp.zeros_like(acc_ref)

    h = h_ref[0]
    a = _wdot(h, wg_ref[0])
    u = _wdot(h, wu_ref[0])
    acc_ref[...] += _wdot(a * jax.nn.sigmoid(a) * u, wd_ref[0])

    @pl.when((e == pl.num_programs(2) - 1) & (f == pl.num_programs(3) - 1))
    def _():
        o_ref[0] = x_ref[0] + gate_ref[0] * acc_ref[...]


def _moe_swiglu_kernel(x_ref, h_ref, comb_ref, wg_ref, wu_ref, wd_ref, gate_ref, o_ref, acc_ref):
    e, f = pl.program_id(2), pl.program_id(3)

    @pl.when((e == 0) & (f == 0))
    def _():
        acc_ref[...] = jnp.zeros_like(acc_ref)

    h = h_ref[0]
    a = _wdot(h, wg_ref[0])
    u = _wdot(h, wu_ref[0])
    act = a * jax.nn.sigmoid(a) * u
    comb = comb_ref[0]
    lane = lax.broadcasted_iota(jnp.int32, comb.shape, 1)
    w_e = jnp.sum(jnp.where(lane == e, comb, 0.0), axis=-1, keepdims=True)
    acc_ref[...] += w_e * _wdot(act, wd_ref[0])

    @pl.when((e == pl.num_programs(2) - 1) & (f == pl.num_programs(3) - 1))
    def _():
        o_ref[0] = x_ref[0] + gate_ref[0] * acc_ref[...]


def _swiglu_residual(x, h, w_gate, w_up, w_down, gate, combine=None):
    bn, t, d = x.shape
    n_exp, _, ff = w_gate.shape
    tm = _row_tile(t)
    tf = 1408
    assert ff % tf == 0
    x_spec = pl.BlockSpec((1, tm, d), lambda b, i, e, f: (b, i, 0))
    wi_spec = pl.BlockSpec((1, d, tf), lambda b, i, e, f: (e, 0, f))
    wd_spec = pl.BlockSpec((1, tf, d), lambda b, i, e, f: (e, f, 0))
    if gate.shape[1] == 1:
        gate_spec = pl.BlockSpec((1, 1, d), lambda b, i, e, f: (b, 0, 0))
    else:
        gate_spec = pl.BlockSpec((1, tm, d), lambda b, i, e, f: (b, i, 0))
    if combine is None:
        kern, in_specs, args = _swiglu_kernel, [x_spec, x_spec], [x, h]
    else:
        kern = _moe_swiglu_kernel
        in_specs = [x_spec, x_spec, pl.BlockSpec((1, tm, LANES), lambda b, i, e, f: (b, i, 0))]
        args = [x, h, combine]
    return pl.pallas_call(
        kern, grid=(bn, t // tm, n_exp, ff // tf),
        in_specs=in_specs + [wi_spec, wi_spec, wd_spec, gate_spec],
        out_specs=x_spec, out_shape=jax.ShapeDtypeStruct((bn, t, d), F32),
        scratch_shapes=[pltpu.VMEM((tm, d), F32)],
        compiler_params=_cparams(("arbitrary",) * 4), name="swiglu",
    )(*args, w_gate, w_up, w_down, gate)


def _s5_prep_kernel(are_ref, aim_ref, ldt_ref, bre_ref, bim_ref,
                    pre_ref, pim_ref, obr_ref, obi_ref, *, seg):
    a_re, a_im = are_ref[0], aim_ref[0]
    dt = jnp.exp(ldt_ref[0])
    steps = lax.broadcasted_iota(jnp.int32, (seg, PACK_STATES), 0).astype(F32) + 1.0
    mag = jnp.exp(steps * (dt * a_re))
    ang = steps * (dt * a_im)
    pre_ref[0] = mag * jnp.cos(ang)
    pim_ref[0] = mag * jnp.sin(ang)
    mag1 = jnp.exp(dt * a_re)
    ab_re = mag1 * jnp.cos(dt * a_im)
    ab_im = mag1 * jnp.sin(dt * a_im)
    den = a_re * a_re + a_im * a_im
    num_re = ab_re - 1.0
    coef_re = (num_re * a_re + ab_im * a_im) / den
    coef_im = (ab_im * a_re - num_re * a_im) / den
    b_re, b_im = bre_ref[0, 0], bim_ref[0, 0]
    obr_ref[0, 0] = coef_re * b_re - coef_im * b_im
    obi_ref[0, 0] = coef_re * b_im + coef_im * b_re


def _blockdiag_b(b):
    nl, g, p, j = b.shape
    b5 = b.reshape(nl, g // PACK_GROUPS, PACK_GROUPS, p, j).transpose(0, 1, 2, 4, 3)
    eye = jnp.eye(PACK_GROUPS, dtype=b.dtype)
    out = b5[:, :, :, :, None, :] * eye[None, None, :, None, :, None]
    return out.reshape(nl, g // PACK_GROUPS, PACK_GROUPS * j, PACK_GROUPS * p)


def _blockdiag_c(c):
    nl, g, j, p = c.shape
    c5 = c.reshape(nl, g // PACK_GROUPS, PACK_GROUPS, j, p).transpose(0, 1, 2, 4, 3)
    eye = jnp.eye(PACK_GROUPS, dtype=c.dtype)
    out = c5[:, :, :, :, None, :] * eye[None, None, :, None, :, None]
    return out.reshape(nl, g // PACK_GROUPS, PACK_GROUPS * p, PACK_GROUPS * j)


def _s5_prep(a_re, a_im, log_dt, b_re, b_im, seg):
    nl, g, p = a_re.shape
    n_state = g * p
    n_pack = g // PACK_GROUPS
    row = lambda a: a.reshape(nl, 1, n_state)
    ldt = jnp.broadcast_to(log_dt[:, :, None], (nl, g, p))
    braw_re, braw_im = _blockdiag_b(b_re), _blockdiag_b(b_im)
    vec_spec = pl.BlockSpec((1, 1, PACK_STATES), lambda l, k: (l, 0, k))
    mat_spec = pl.BlockSpec((1, 1, LANES, PACK_STATES), lambda l, k: (l, k, 0, 0))
    pow_spec = pl.BlockSpec((1, seg, PACK_STATES), lambda l, k: (l, 0, k))
    return pl.pallas_call(
        functools.partial(_s5_prep_kernel, seg=seg),
        grid=(nl, n_pack),
        in_specs=[vec_spec, vec_spec, vec_spec, mat_spec, mat_spec],
        out_specs=[pow_spec, pow_spec, mat_spec, mat_spec],
        out_shape=[jax.ShapeDtypeStruct((nl, seg, n_state), F32)] * 2
        + [jax.ShapeDtypeStruct(braw_re.shape, F32)] * 2,
        compiler_params=_cparams(("arbitrary", "arbitrary")), name="s5_prep",
    )(row(a_re), row(a_im), row(ldt), braw_re, braw_im)


def _s5_prompt_kernel(u_ref, bb_ref, cre_ref, cim_ref, pre_ref, pim_ref, d_ref,
                      g_ref, fre_ref, fim_ref,
                      uperm, sre, sim, gperm, car_re, car_im, ini_re, ini_im, *, seg, chunk):
    n_pack = bb_ref.shape[0]
    n_state = sre.shape[1]

    @pl.when(pl.program_id(1) == 0)
    def _():
        car_re[...] = jnp.zeros_like(car_re)
        car_im[...] = jnp.zeros_like(car_im)

    for pk in range(n_pack):
        for k in range(SUBLANES):
            uperm[pk, pl.ds(k, seg, stride=SUBLANES), :] = (
                u_ref[0, k * seg:(k + 1) * seg, pk * LANES:(pk + 1) * LANES])

    for pk in range(n_pack):
        z = jnp.dot(uperm[pk].astype(BF16), bb_ref[pk], preferred_element_type=F32)
        sre[:, pk * PACK_STATES:(pk + 1) * PACK_STATES] = z[:, :PACK_STATES]
        sim[:, pk * PACK_STATES:(pk + 1) * PACK_STATES] = z[:, PACK_STATES:]

    for c in range(n_state // chunk):
        cs = slice(c * chunk, (c + 1) * chunk)
        lam_re = jnp.broadcast_to(pre_ref[0:1, cs], (SUBLANES, chunk))
        lam_im = jnp.broadcast_to(pim_ref[0:1, cs], (SUBLANES, chunk))

        def scan_step(n, carry, cs=cs, lam_re=lam_re, lam_im=lam_im):
            h_re, h_im = carry
            rows = pl.ds(pl.multiple_of(n * SUBLANES, SUBLANES), SUBLANES)
            n_re = lam_re * h_re - lam_im * h_im + sre[rows, cs]
            n_im = lam_re * h_im + lam_im * h_re + sim[rows, cs]
            sre[rows, cs] = n_re
            sim[rows, cs] = n_im
            return n_re, n_im

        zero = jnp.zeros((SUBLANES, chunk), F32)
        end_re, end_im = lax.fori_loop(0, seg, scan_step, (zero, zero), unroll=4)

        seg_re, seg_im = pre_ref[seg - 1:seg, cs], pim_ref[seg - 1:seg, cs]
        cur_re, cur_im = car_re[:, cs], car_im[:, cs]
        for k in range(SUBLANES):
            ini_re[k:k + 1, cs] = cur_re
            ini_im[k:k + 1, cs] = cur_im
            nxt_re = end_re[k:k + 1] + seg_re * cur_re - seg_im * cur_im
            nxt_im = end_im[k:k + 1] + seg_re * cur_im + seg_im * cur_re
            cur_re, cur_im = nxt_re, nxt_im
        car_re[:, cs] = cur_re
        car_im[:, cs] = cur_im

        in_re, in_im = ini_re[:, cs], ini_im[:, cs]

        def fix_step(n, carry, cs=cs, in_re=in_re, in_im=in_im):
            rows = pl.ds(pl.multiple_of(n * SUBLANES, SUBLANES), SUBLANES)
            p_re, p_im = pre_ref[pl.ds(n, 1), cs], pim_ref[pl.ds(n, 1), cs]
            sre[rows, cs] = sre[rows, cs] + p_re * in_re - p_im * in_im
            sim[rows, cs] = sim[rows, cs] + p_re * in_im + p_im * in_re
            return carry

        lax.fori_loop(0, seg, fix_step, 0, unroll=4)

    fre_ref[0] = car_re[...]
    fim_ref[0] = car_im[...]

    for pk in range(n_pack):
        ss = slice(pk * PACK_STATES, (pk + 1) * PACK_STATES)
        ls = slice(pk * LANES, (pk + 1) * LANES)
        y = (jnp.dot(sre[:, ss].astype(BF16), cre_ref[pk], preferred_element_type=F32)
             - jnp.dot(sim[:, ss].astype(BF16), cim_ref[pk], preferred_element_type=F32))
        y = y + d_ref[:, ls] * uperm[pk]
        gperm[pk] = jax.nn.gelu(y, approximate=True)

    for pk in range(n_pack):
        for k in range(SUBLANES):
            g_ref[0, k * seg:(k + 1) * seg, pk * LANES:(pk + 1) * LANES] = (
                gperm[pk, pl.ds(k, seg, stride=SUBLANES), :].astype(g_ref.dtype))


def _s5_prompt(u, bb16, cre16, cim16, pow_re, pow_im, d_skip):
    bn, t, d = u.shape
    n_pack = bb16.shape[0]
    n_state = pow_re.shape[1]
    tt = S5_TILE
    seg = tt // SUBLANES
    assert t % tt == 0 and pow_re.shape[0] == seg
    const = lambda shape: pl.BlockSpec(shape, lambda b, i: (0,) * len(shape))
    u_spec = pl.BlockSpec((1, tt, d), lambda b, i: (b, i, 0))
    fin_spec = pl.BlockSpec((1, 1, n_state), lambda b, i: (b, 0, 0))
    return pl.pallas_call(
        functools.partial(_s5_prompt_kernel, seg=seg, chunk=512),
        grid=(bn, t // tt),
        in_specs=[u_spec, const(bb16.shape), const(cre16.shape), const(cim16.shape),
                  const(pow_re.shape), const(pow_im.shape), const((1, d))],
        out_specs=[u_spec, fin_spec, fin_spec],
        out_shape=[jax.ShapeDtypeStruct((bn, t, d), BF16),
                   jax.ShapeDtypeStruct((bn, 1, n_state), F32),
                   jax.ShapeDtypeStruct((bn, 1, n_state), F32)],
        scratch_shapes=[pltpu.VMEM((n_pack, tt, LANES), F32), pltpu.VMEM((tt, n_state), F32),
                        pltpu.VMEM((tt, n_state), F32), pltpu.VMEM((n_pack, tt, LANES), F32),
                        pltpu.VMEM((1, n_state), F32), pltpu.VMEM((1, n_state), F32),
                        pltpu.VMEM((SUBLANES, n_state), F32), pltpu.VMEM((SUBLANES, n_state), F32)],
        compiler_params=_cparams(("arbitrary", "arbitrary")), name="s5_prompt",
    )(u, bb16, cre16, cim16, pow_re, pow_im, d_skip.reshape(1, d))


def _s5_sample_kernel(u_ref, hre_ref, him_ref, bre_ref, bim_ref, cre_ref, cim_ref,
                      lre_ref, lim_ref, d_ref, g_ref, sre_ref, sim_ref):
    for pk in range(bre_ref.shape[0]):
        ss = slice(pk * PACK_STATES, (pk + 1) * PACK_STATES)
        ls = slice(pk * LANES, (pk + 1) * LANES)
        u = u_ref[:, ls]
        lam_re, lam_im = lre_ref[:, ss], lim_ref[:, ss]
        h_re, h_im = hre_ref[:, ss], him_ref[:, ss]
        s_re = (jnp.dot(u, bre_ref[pk], precision=HI, preferred_element_type=F32)
                + (lam_re * h_re - lam_im * h_im))
        s_im = (jnp.dot(u, bim_ref[pk], precision=HI, preferred_element_type=F32)
                + (lam_re * h_im + lam_im * h_re))
        sre_ref[:, ss] = s_re
        sim_ref[:, ss] = s_im
        y = (jnp.dot(s_re, cre_ref[pk], precision=HI, preferred_element_type=F32)
             - jnp.dot(s_im, cim_ref[pk], precision=HI, preferred_element_type=F32))
        y = y + d_ref[:, ls] * u
        g_ref[:, ls] = jax.nn.gelu(y, approximate=True).astype(g_ref.dtype)


def _s5_sample(u, h_re, h_im, bbar_re, bbar_im, c_re, c_im, lam_re, lam_im, d_skip):
    rows, d = u.shape
    n_state = h_re.shape[1]
    return pl.pallas_call(
        _s5_sample_kernel,
        out_shape=[jax.ShapeDtypeStruct((rows, d), F32),
                   jax.ShapeDtypeStruct((rows, n_state), F32),
                   jax.ShapeDtypeStruct((rows, n_state), F32)],
        compiler_params=pltpu.CompilerParams(vmem_limit_bytes=VMEM_LIMIT), name="s5_sample",
    )(u, h_re, h_im, bbar_re, bbar_im, c_re, c_im, lam_re, lam_im, d_skip.reshape(1, d))


def _top_blocks_penalty(gate, blk, n_valid_below):
    gate = jnp.where(blk < n_valid_below, gate, NEG_INF)
    n_blk = gate.shape[0]
    sel = jnp.zeros(gate.shape, F32)
    for _ in range(MOBA_TOPK):
        m = jnp.max(gate, axis=0, keepdims=True)
        idx = jnp.min(jnp.where(gate == m, blk, n_blk), axis=0, keepdims=True)
        hit = blk == idx
        sel = jnp.where(hit & (m > NEG_INF), 1.0, sel)
        gate = jnp.where(hit, NEG_INF, gate)
    return jnp.where(sel > 0.0, 0.0, NEG_INF)


def _qkv_prompt_kernel(h_ref, w_ref, k_ref, v_ref, k16_ref, qt_ref, vt_ref, pen_ref, kmean):
    i = pl.program_id(1)
    d = k_ref.shape[-1]
    n_blk = kmean.shape[0]

    @pl.when(i == 0)
    def _():
        kmean[...] = jnp.zeros_like(kmean)

    r = jnp.dot(h_ref[0], w_ref[...], preferred_element_type=F32)
    q, k, v = r[:, :d], r[:, d:2 * d], r[:, 2 * d:]
    k_ref[0] = k
    v_ref[0] = v
    k16_ref[0] = k.astype(BF16)
    q_t = q.T
    qt_ref[0] = (q_t * (HEAD_DIM ** -0.5)).astype(BF16)
    vt_ref[0, 0] = v.T.astype(BF16)

    km = kmean[...]
    blk = lax.broadcasted_iota(jnp.int32, (n_blk, MOBA_BLOCK), 0)
    for h in range(N_HEADS):
        hs = slice(h * HEAD_DIM, (h + 1) * HEAD_DIM)
        gate = jnp.dot(km[:, hs], q_t[hs, :], precision=HI, preferred_element_type=F32)
        pen_ref[0, h] = _top_blocks_penalty(gate, blk, i)

    row = lax.broadcasted_iota(jnp.int32, km.shape, 0)
    kmean[...] = jnp.where(row == i, jnp.mean(k, axis=0, keepdims=True), km)


def _qkv_prompt(h16, w16):
    bn, t, d = h16.shape
    n_blk = t // MOBA_BLOCK
    tile = pl.BlockSpec((1, MOBA_BLOCK, d), lambda b, i: (b, i, 0))
    return pl.pallas_call(
        _qkv_prompt_kernel, grid=(bn, n_blk),
        in_specs=[tile, pl.BlockSpec(w16.shape, lambda b, i: (0, 0))],
        out_specs=[tile, tile, tile,
                   pl.BlockSpec((1, d, MOBA_BLOCK), lambda b, i: (b, 0, i)),
                   pl.BlockSpec((1, 1, d, MOBA_BLOCK), lambda b, i: (b, i, 0, 0)),
                   pl.BlockSpec((1, N_HEADS, n_blk, MOBA_BLOCK), lambda b, i: (b, 0, 0, i))],
        out_shape=[jax.ShapeDtypeStruct((bn, t, d), F32), jax.ShapeDtypeStruct((bn, t, d), F32),
                   jax.ShapeDtypeStruct((bn, t, d), BF16),
                   jax.ShapeDtypeStruct((bn, d, t), BF16),
                   jax.ShapeDtypeStruct((bn, n_blk, d, MOBA_BLOCK), BF16),
                   jax.ShapeDtypeStruct((bn, N_HEADS, n_blk, t), F32)],
        scratch_shapes=[pltpu.VMEM((n_blk, d), F32)],
        compiler_params=_cparams(("arbitrary", "arbitrary")), name="qkv_prompt",
    )(h16, w16)


def _moba_prompt_kernel(slope_ref, qt_ref, k_ref, vt_ref, pen_ref, o_ref):
    hp, i = pl.program_id(1), pl.program_id(2)
    blk = MOBA_BLOCK
    k_iota = lax.broadcasted_iota(jnp.int32, (blk, blk), 0)
    q_iota = lax.broadcasted_iota(jnp.int32, (blk, blk), 1)
    causal = k_iota <= q_iota
    k_local = k_iota.astype(F32)
    head_row = lax.broadcasted_iota(jnp.int32, (2 * HEAD_DIM, blk), 0) // HEAD_DIM
    q_all = qt_ref[0]
    outs = []
    for a in range(2):
        slope = slope_ref[2 * hp + a]
        q_a = jnp.where(head_row == a, q_all, jnp.zeros_like(q_all))
        k_bias = slope * k_local
        rows_a = slice(a * HEAD_DIM, (a + 1) * HEAD_DIM)

        k_d = k_ref[0, pl.ds(pl.multiple_of(i * blk, blk), blk), :]
        s = jnp.dot(k_d, q_a, preferred_element_type=F32) + k_bias
        s = jnp.where(causal, s, NEG_INF)
        m0 = jnp.max(s, axis=0, keepdims=True)
        p = jnp.exp(s - m0)
        l0 = jnp.sum(p, axis=0, keepdims=True)
        acc0 = jnp.dot(vt_ref[0, i, rows_a, :], p.astype(BF16), preferred_element_type=F32)

        def body(n, carry, a=a, q_a=q_a, k_bias=k_bias, slope=slope, rows_a=rows_a):
            m, l, acc = carry
            k_n = k_ref[0, pl.ds(pl.multiple_of(n * blk, blk), blk), :]
            s = jnp.dot(k_n, q_a, preferred_element_type=F32) + k_bias
            pen = pen_ref[0, a, pl.ds(n, 1), :]
            shift = slope * jnp.full((1, blk), (n - i) * blk, jnp.int32).astype(F32)
            m_new = jnp.maximum(m, jnp.max(s, axis=0, keepdims=True) + shift + pen)
            p = jnp.exp(s - (m_new - shift - pen))
            alpha = jnp.exp(m - m_new)
            l = alpha * l + jnp.sum(p, axis=0, keepdims=True)
            acc = alpha * acc + jnp.dot(vt_ref[0, n, rows_a, :], p.astype(BF16),
                                        preferred_element_type=F32)
            return m_new, l, acc

        _, l, acc = lax.fori_loop(0, i, body, (m0, l0, acc0))
        outs.append(acc / l)
    o_ref[0] = jnp.concatenate(outs, axis=0).T.astype(o_ref.dtype)


def _alibi_slopes():
    return 2.0 ** (-8.0 * jnp.arange(1, N_HEADS + 1, dtype=F32) / N_HEADS)


def _moba_prompt(qt16, k16, vt16, pen):
    bn, d, t = qt16.shape
    n_blk = t // MOBA_BLOCK
    pair = 2 * HEAD_DIM
    return pl.pallas_call(
        _moba_prompt_kernel,
        grid_spec=pltpu.PrefetchScalarGridSpec(
            num_scalar_prefetch=0,
            grid=(bn, N_HEADS // 2, n_blk),
            in_specs=[pl.BlockSpec(memory_space=pltpu.SMEM),
                      pl.BlockSpec((1, pair, MOBA_BLOCK), lambda b, hp, i: (b, hp, i)),
                      pl.BlockSpec((1, t, pair), lambda b, hp, i: (b, 0, hp)),
                      pl.BlockSpec((1, n_blk, pair, MOBA_BLOCK), lambda b, hp, i: (b, 0, hp, 0)),
                      pl.BlockSpec((1, 2, n_blk, MOBA_BLOCK), lambda b, hp, i: (b, hp, 0, i))],
            out_specs=pl.BlockSpec((1, MOBA_BLOCK, pair), lambda b, hp, i: (b, i, hp)),
        ),
        out_shape=jax.ShapeDtypeStruct((bn, t, d), BF16),
        compiler_params=_cparams(("arbitrary",) * 3), name="moba_prompt",
    )(_alibi_slopes(), qt16, k16, vt16, pen)


def _matmul_kernel(a_ref, w_ref, o_ref):
    o_ref[...] = _wdot(a_ref[...], w_ref[...])


def _matmul(a, w):
    return pl.pallas_call(
        _matmul_kernel, out_shape=jax.ShapeDtypeStruct((a.shape[0], w.shape[1]), F32),
        compiler_params=pltpu.CompilerParams(vmem_limit_bytes=VMEM_LIMIT), name="matmul",
    )(a, w)


def _moba_sample_kernel(pt_ref, q_ref, kn_ref, vn_ref, slope_ref, k0_ref, k1_ref, v0_ref, v1_ref,
                        o_ref, qbd, m_s, l_s, g_s, o_s, *, past_len):
    del pt_ref
    n = pl.program_id(1)
    n_blk = pl.num_programs(1)
    d = q_ref.shape[-1]
    head_of_lane = lax.broadcasted_iota(jnp.int32, (N_HEADS, d), 1) // HEAD_DIM
    head_mask = head_of_lane == lax.broadcasted_iota(jnp.int32, (N_HEADS, d), 0)
    lane = lax.broadcasted_iota(jnp.int32, (N_HEADS, LANES), 1)

    @pl.when(n == 0)
    def _():
        qbd[...] = jnp.where(head_mask, q_ref[0], 0.0)
        m_s[...] = jnp.zeros_like(m_s)
        l_s[...] = jnp.zeros_like(l_s)
        g_s[...] = jnp.zeros_like(g_s)

    q_bd = qbd[...]
    k_blk = jnp.concatenate([k0_ref[0, 0], k1_ref[0, 0]], axis=0)
    v_blk = jnp.concatenate([v0_ref[0, 0], v1_ref[0, 0]], axis=0)
    k_mean = jnp.mean(k_blk, axis=0, keepdims=True)
    gate = jnp.sum(q_bd * k_mean, axis=-1, keepdims=True)

    s = lax.dot_general((q_bd * (HEAD_DIM ** -0.5)).astype(BF16), k_blk.astype(BF16),
                        (((1,), (1,)), ((), ())), preferred_element_type=F32)
    k_pos = n * MOBA_BLOCK + lax.broadcasted_iota(jnp.int32, s.shape, 1)
    s = s - slope_ref[...] * (past_len - k_pos).astype(F32)
    m = jnp.max(s, axis=-1, keepdims=True)
    p = jnp.exp(s - m)
    l = jnp.sum(p, axis=-1, keepdims=True)
    o_s[n] = jnp.dot(p.astype(BF16), v_blk.astype(BF16), preferred_element_type=F32)
    m_s[...] = jnp.where(lane == n, m, m_s[...])
    l_s[...] = jnp.where(lane == n, l, l_s[...])
    g_s[...] = jnp.where(lane == n, gate, g_s[...])

    @pl.when(n == n_blk - 1)
    def _():
        gates = jnp.where(lane < n_blk, g_s[...], NEG_INF)
        sel = jnp.zeros(gates.shape, F32)
        for _ in range(MOBA_TOPK):
            g_max = jnp.max(gates, axis=-1, keepdims=True)
            idx = jnp.min(jnp.where(gates == g_max, lane, LANES), axis=-1, keepdims=True)
            hit = lane == idx
            sel = jnp.where(hit & (g_max > NEG_INF), 1.0, sel)
            gates = jnp.where(hit, NEG_INF, gates)
        chosen = sel > 0.0
        s_own = jnp.sum(q_bd * kn_ref[0], axis=-1, keepdims=True) * (HEAD_DIM ** -0.5)
        m_all = m_s[...]
        m_tot = jnp.maximum(jnp.max(jnp.where(chosen, m_all, NEG_INF), axis=-1, keepdims=True), s_own)
        w = jnp.where(chosen, jnp.exp(m_all - m_tot), 0.0)
        w_own = jnp.exp(s_own - m_tot)
        l_tot = jnp.sum(w * l_s[...], axis=-1, keepdims=True) + w_own
        acc = w_own * vn_ref[0]
        for j in range(o_s.shape[0]):
            acc = acc + w[:, j:j + 1] * o_s[j]
        o_ref[0] = jnp.sum(jnp.where(head_mask, acc / l_tot, 0.0), axis=0, keepdims=True).astype(o_ref.dtype)


def _moba_sample(q, k_new, v_new, cache_k, cache_v, page_table, layer):
    dec, _, d = q.shape
    n_pages = page_table.shape[1]
    page = cache_k.shape[2]
    assert MOBA_BLOCK == 2 * page
    n_blk = n_pages // 2
    assert n_blk <= LANES
    past_len = n_pages * page
    vec = pl.BlockSpec((1, 1, d), lambda b, n, pt: (b, 0, 0))

    def page_spec(half):
        return pl.BlockSpec((1, 1, page, d), lambda b, n, pt: (layer, pt[b, 2 * n + half], 0, 0))

    return pl.pallas_call(
        functools.partial(_moba_sample_kernel, past_len=past_len),
        grid_spec=pltpu.PrefetchScalarGridSpec(
            num_scalar_prefetch=1,
            grid=(dec, n_blk),
            in_specs=[vec, vec, vec, pl.BlockSpec((N_HEADS, 1), lambda b, n, pt: (0, 0)),
                      page_spec(0), page_spec(1), page_spec(0), page_spec(1)],
            out_specs=vec,
            scratch_shapes=[pltpu.VMEM((N_HEADS, d), F32), pltpu.VMEM((N_HEADS, LANES), F32),
                            pltpu.VMEM((N_HEADS, LANES), F32), pltpu.VMEM((N_HEADS, LANES), F32),
                            pltpu.VMEM((n_blk, N_HEADS, d), F32)],
        ),
        out_shape=jax.ShapeDtypeStruct((dec, 1, d), F32),
        compiler_params=_cparams(("arbitrary", "arbitrary")), name="moba_sample",
    )(page_table, q, k_new, v_new, _alibi_slopes().reshape(N_HEADS, 1), cache_k, cache_k, cache_v, cache_v)


def _trunk(x, mods, p, past):
    bn, t, d = x.shape
    ssm_re, ssm_im, new_k, new_v = [], [], [], []
    depth = len(mods)
    for layer in range(depth):
        i = layer // 2
        sh1, sc1, g1, sh2, sc2, g2 = mods[layer]
        act_dtype = p['attn_w_o'].dtype
        if layer % 2 == 0:
            u = _modulate(x, p['norm_mix'][layer], sh1, sc1, F32)
            if past is None:
                g, f_re, f_im = _s5_prompt(u, p['s5_bb16'][i], p['s5_cre16'][i], p['s5_cim16'][i],
                                           p['s5_pow_re'][i], p['s5_pow_im'][i], p['ssm_d'][i])
                ssm_re.append(f_re.reshape(bn, -1, SSM_STATE))
                ssm_im.append(f_im.reshape(bn, -1, SSM_STATE))
            else:
                h_re = past[3][i].reshape(t, -1)
                h_im = past[4][i].reshape(t, -1)
                g, s_re, s_im = _s5_sample(u[0], h_re, h_im, p['s5_bbar_re'][i], p['s5_bbar_im'][i],
                                           p['s5_c_re'][i], p['s5_c_im'][i],
                                           p['s5_pow_re'][i][0:1], p['s5_pow_im'][i][0:1], p['ssm_d'][i])
                g = g[None]
                ssm_re.append(s_re.reshape(t, -1, SSM_STATE))
                ssm_im.append(s_im.reshape(t, -1, SSM_STATE))
            x = _proj_residual(x, g, (p['ssm_w_glu_out'][i], p['ssm_w_glu_gate'][i]), g1)
            h = _modulate(x, p['norm_ffn'][layer], sh2, sc2, act_dtype)
            x = _swiglu_residual(x, h, p['ffn_w_gate'][i:i + 1], p['ffn_w_up'][i:i + 1],
                                 p['ffn_w_down'][i:i + 1], g2)
        else:
            h = _modulate(x, p['norm_mix'][layer], sh1, sc1, act_dtype)
            if past is None:
                k, v, k16, qt16, vt16, pen = _qkv_prompt(h, p['attn_w_qkv'][i])
                a = _moba_prompt(qt16, k16, vt16, pen)
                new_k.append(k.reshape(bn, t, N_HEADS, HEAD_DIM))
                new_v.append(v.reshape(bn, t, N_HEADS, HEAD_DIM))
            else:
                qkv = _matmul(h[0], p['attn_w_qkv'][i])
                q, k, v = (qkv[:, j * d:(j + 1) * d].reshape(t, 1, d) for j in range(3))
                a = _moba_sample(q, k, v, past[0], past[1], past[2], i).reshape(1, t, d)
                new_k.append(k.reshape(t, 1, N_HEADS, HEAD_DIM))
                new_v.append(v.reshape(t, 1, N_HEADS, HEAD_DIM))
            x = _proj_residual(x, a, (p['attn_w_o'][i],), g1)
            h, comb = _modulate(x, p['norm_ffn'][layer], sh2, sc2, act_dtype,
                                router=(p['moe_w_router'][i], p['moe_b_router'][i]))
            x = _swiglu_residual(x, h, p['moe_w_gate'][i], p['moe_w_up'][i], p['moe_w_down'][i],
                                 g2, combine=comb)
    y = _final_norm(x, p['norm_final'])
    return y, jnp.stack(ssm_re), jnp.stack(ssm_im), jnp.stack(new_k), jnp.stack(new_v)


def kernel(x_prompt, x_sample, cache_k, cache_v, page_table, state_ssm_re, state_ssm_im,
           c_prompt, c_sample, ada_w, ada_b, norm_mix, norm_ffn, norm_final,
           ssm_a_re, ssm_a_im, ssm_log_dt, ssm_b_re, ssm_b_im, ssm_c_re, ssm_c_im, ssm_d,
           ssm_w_glu_out, ssm_w_glu_gate, attn_w_qkv, attn_w_o,
           ffn_w_gate, ffn_w_up, ffn_w_down,
           moe_w_router, moe_b_router, moe_w_gate, moe_w_up, moe_w_down):
    bn, t, d = x_prompt.shape
    dec = x_sample.shape[0]
    depth = ada_w.shape[0]
    assert x_sample.shape[1] == 1 and d == N_HEADS * HEAD_DIM and t % MOBA_BLOCK == 0

    n_cond = bn + dec
    c_all = jnp.pad(jnp.concatenate([c_prompt, c_sample], axis=0), ((0, -n_cond % SUBLANES), (0, 0)))
    mod = _adaln(c_all, ada_w, ada_b)
    mods_p = [[mod[l, :bn, j * d:(j + 1) * d].reshape(bn, 1, d) for j in range(6)] for l in range(depth)]
    mods_s = [[mod[l, bn:n_cond, j * d:(j + 1) * d].reshape(1, dec, d) for j in range(6)] for l in range(depth)]

    pow_re, pow_im, bbar_re, bbar_im = _s5_prep(ssm_a_re, ssm_a_im, ssm_log_dt, ssm_b_re, ssm_b_im,
                                                S5_TILE // SUBLANES)
    c_re_bd, c_im_bd = _blockdiag_c(ssm_c_re), _blockdiag_c(ssm_c_im)
    shared = {
        'norm_mix': norm_mix, 'norm_ffn': norm_ffn, 'norm_final': norm_final, 'ssm_d': ssm_d,
        's5_pow_re': pow_re, 's5_pow_im': pow_im, 's5_bbar_re': bbar_re, 's5_bbar_im': bbar_im,
        's5_bb16': jnp.concatenate([bbar_re, bbar_im], axis=-1).astype(BF16),
        's5_c_re': c_re_bd, 's5_c_im': c_im_bd,
        's5_cre16': c_re_bd.astype(BF16), 's5_cim16': c_im_bd.astype(BF16),
        'moe_w_router': moe_w_router, 'moe_b_router': moe_b_router,
    }
    big = {
        'ssm_w_glu_out': ssm_w_glu_out, 'ssm_w_glu_gate': ssm_w_glu_gate,
        'attn_w_qkv': attn_w_qkv, 'attn_w_o': attn_w_o,
        'ffn_w_gate': ffn_w_gate, 'ffn_w_up': ffn_w_up, 'ffn_w_down': ffn_w_down,
        'moe_w_gate': moe_w_gate, 'moe_w_up': moe_w_up, 'moe_w_down': moe_w_down,
    }
    p_prompt = dict(shared, **{name: w.astype(BF16) for name, w in big.items()})
    p_sample = dict(shared, **big)

    y_p, re_p, im_p, k_p, v_p = _trunk(x_prompt, mods_p, p_prompt, None)

    n_layers, n_phys, page = cache_k.shape[:3]
    past = (cache_k.reshape(n_layers, n_phys, page, d), cache_v.reshape(n_layers, n_phys, page, d),
            page_table, state_ssm_re, state_ssm_im)
    y_s, re_s, im_s, k_s, v_s = _trunk(x_sample.reshape(1, dec, d), mods_s, p_sample, past)
    return (y_p, y_s.reshape(dec, 1, d), re_p, im_p, re_s, im_s, k_p, v_p, k_s, v_s)
```

```python
import functools

import jax
import jax.numpy as jnp
from jax import lax
from jax.experimental import pallas as pl
from jax.experimental.pallas import tpu as pltpu

F32 = jnp.float32
BF16 = jnp.bfloat16
HI = lax.Precision.HIGHEST
NEG_INF = float("-inf")
LOG2E = 1.4426950408889634

SUBLANES = 8
LANES = 128

RMS_EPS = 1e-6
SSM_GROUP = 16
SSM_STATE = 64
PACK_GROUPS = LANES // SSM_GROUP
PACK_STATES = PACK_GROUPS * SSM_STATE
N_HEADS = 16
HEAD_DIM = 64
MOBA_BLOCK = 256
MOBA_TOPK = 3
TOP_K = 2

ROW_TILE = 512
S5_TILE = 256
VMEM_LIMIT = 56 * 1024 * 1024


def _cparams(sem):
    return pltpu.CompilerParams(dimension_semantics=sem, vmem_limit_bytes=VMEM_LIMIT)


def _wdot(a, w):
    if w.dtype == F32:
        return jnp.dot(a.astype(F32), w, precision=HI, preferred_element_type=F32)
    return jnp.dot(a.astype(BF16), w, preferred_element_type=F32)


def _row_tile(t):
    return ROW_TILE if t % ROW_TILE == 0 else t


def _mod_spec(arr, tm):
    d = arr.shape[-1]
    if arr.shape[1] == 1:
        return pl.BlockSpec((1, 1, d), lambda b, t: (b, 0, 0))
    return pl.BlockSpec((1, tm, d), lambda b, t: (b, t, 0))


def _adaln_kernel(c_ref, w_ref, b_ref, o_ref):
    c = c_ref[...]
    s = c * jax.nn.sigmoid(c)
    o_ref[0] = jnp.dot(s, w_ref[0], precision=HI, preferred_element_type=F32) + b_ref[0]


def _adaln(c_all, ada_w, ada_b):
    depth, d, n = ada_w.shape
    r = c_all.shape[0]
    tn = 1536
    return pl.pallas_call(
        _adaln_kernel,
        grid=(depth, n // tn),
        in_specs=[pl.BlockSpec((r, d), lambda l, j: (0, 0)),
                  pl.BlockSpec((1, d, tn), lambda l, j: (l, 0, j)),
                  pl.BlockSpec((1, 1, tn), lambda l, j: (l, 0, j))],
        out_specs=pl.BlockSpec((1, r, tn), lambda l, j: (l, 0, j)),
        out_shape=jax.ShapeDtypeStruct((depth, r, n), F32),
        compiler_params=_cparams(("arbitrary", "arbitrary")),
        name="adaln",
    )(c_all, ada_w, ada_b.reshape(depth, 1, n))


def _rms(x, g):
    return x * lax.rsqrt(jnp.mean(x * x, axis=-1, keepdims=True) + RMS_EPS) * g


def _modulate_kernel(x_ref, g_ref, sh_ref, sc_ref, o_ref):
    h = _rms(x_ref[0], g_ref[...]) * (1.0 + sc_ref[0]) + sh_ref[0]
    o_ref[0] = h.astype(o_ref.dtype)


def _modulate_router_kernel(x_ref, g_ref, sh_ref, sc_ref, wr_ref, br_ref, o_ref, comb_ref):
    h = _rms(x_ref[0], g_ref[...]) * (1.0 + sc_ref[0]) + sh_ref[0]
    o_ref[0] = h.astype(o_ref.dtype)
    logits = jnp.dot(h, wr_ref[...], precision=HI, preferred_element_type=F32) + br_ref[...]
    lane = lax.broadcasted_iota(jnp.int32, logits.shape, 1)
    m1 = jnp.max(logits, axis=-1, keepdims=True)
    i1 = jnp.min(jnp.where(logits == m1, lane, LANES), axis=-1, keepdims=True)
    rest = jnp.where(lane == i1, NEG_INF, logits)
    m2 = jnp.max(rest, axis=-1, keepdims=True)
    i2 = jnp.min(jnp.where(rest == m2, lane, LANES), axis=-1, keepdims=True)
    e2 = jnp.exp(m2 - m1)
    den = 1.0 + e2
    comb_ref[0] = jnp.where(lane == i1, 1.0 / den, 0.0) + jnp.where(lane == i2, e2 / den, 0.0)


def _modulate(x, g, shift, scale, out_dtype, router=None):
    bn, t, d = x.shape
    tm = _row_tile(t)
    grid = (bn, t // tm)
    x_spec = pl.BlockSpec((1, tm, d), lambda b, i: (b, i, 0))
    in_specs = [x_spec, pl.BlockSpec((1, d), lambda b, i: (0, 0)), _mod_spec(shift, tm), _mod_spec(scale, tm)]
    args = [x, g.reshape(1, d), shift, scale]
    if router is None:
        return pl.pallas_call(
            _modulate_kernel, grid=grid, in_specs=in_specs, out_specs=x_spec,
            out_shape=jax.ShapeDtypeStruct((bn, t, d), out_dtype),
            compiler_params=_cparams(("arbitrary", "arbitrary")), name="modulate",
        )(*args)
    w_router, b_router = router
    n_exp = w_router.shape[1]
    wr = jnp.pad(w_router, ((0, 0), (0, LANES - n_exp)))
    br = jnp.pad(b_router.reshape(1, n_exp), ((0, 0), (0, LANES - n_exp)), constant_values=NEG_INF)
    in_specs += [pl.BlockSpec((d, LANES), lambda b, i: (0, 0)), pl.BlockSpec((1, LANES), lambda b, i: (0, 0))]
    return pl.pallas_call(
        _modulate_router_kernel, grid=grid, in_specs=in_specs,
        out_specs=[x_spec, pl.BlockSpec((1, tm, LANES), lambda b, i: (b, i, 0))],
        out_shape=[jax.ShapeDtypeStruct((bn, t, d), out_dtype), jax.ShapeDtypeStruct((bn, t, LANES), F32)],
        compiler_params=_cparams(("arbitrary", "arbitrary")), name="modulate_router",
    )(*args, wr, br)


def _final_norm_kernel(x_ref, g_ref, o_ref):
    o_ref[0] = _rms(x_ref[0], g_ref[...])


def _final_norm(x, g):
    bn, t, d = x.shape
    tm = _row_tile(t)
    x_spec = pl.BlockSpec((1, tm, d), lambda b, i: (b, i, 0))
    return pl.pallas_call(
        _final_norm_kernel, grid=(bn, t // tm),
        in_specs=[x_spec, pl.BlockSpec((1, d), lambda b, i: (0, 0))], out_specs=x_spec,
        out_shape=jax.ShapeDtypeStruct((bn, t, d), F32),
        compiler_params=_cparams(("arbitrary", "arbitrary")), name="final_norm",
    )(x, g.reshape(1, d))


def _proj_residual_kernel(x_ref, a_ref, w_ref, gate_ref, o_ref):
    y = _wdot(a_ref[0], w_ref[...])
    o_ref[0] = x_ref[0] + gate_ref[0] * y


def _glu_residual_kernel(x_ref, a_ref, wo_ref, wg_ref, gate_ref, o_ref):
    a = a_ref[0]
    y = _wdot(a, wo_ref[...])
    z = _wdot(a, wg_ref[...])
    o_ref[0] = x_ref[0] + gate_ref[0] * (y * jax.nn.sigmoid(z))


def _proj_residual(x, a, weights, gate):
    bn, t, d = x.shape
    tm = _row_tile(t)
    x_spec = pl.BlockSpec((1, tm, d), lambda b, i: (b, i, 0))
    w_spec = pl.BlockSpec((d, d), lambda b, i: (0, 0))
    kern = _proj_residual_kernel if len(weights) == 1 else _glu_residual_kernel
    return pl.pallas_call(
        kern, grid=(bn, t // tm),
        in_specs=[x_spec, x_spec] + [w_spec] * len(weights) + [_mod_spec(gate, tm)],
        out_specs=x_spec, out_shape=jax.ShapeDtypeStruct((bn, t, d), F32),
        compiler_params=_cparams(("arbitrary", "arbitrary")), name="proj_residual",
    )(x, a, *weights, gate)


def _swiglu_kernel(x_ref, h_ref, wg_ref, wu_ref, wd_ref, gate_ref, o_ref, acc_ref):
    e, f = pl.program_id(2), pl.program_id(3)

    @pl.when((e == 0) & (f == 0))
    def _():
        acc_ref[...] = jnp.zeros_like(acc_ref)

    h = h_ref[0]
    a = _wdot(h, wg_ref[0])
    u = _wdot(h, wu_ref[0])
    acc_ref[...] += _wdot(a * jax.nn.sigmoid(a) * u, wd_ref[0])

    @pl.when((e == pl.num_programs(2) - 1) & (f == pl.num_programs(3) - 1))
    def _():
        o_ref[0] = x_ref[0] + gate_ref[0] * acc_ref[...]


def _moe_swiglu_kernel(x_ref, h_ref, comb_ref, wg_ref, wu_ref, wd_ref, gate_ref, o_ref, acc_ref):
    e, f = pl.program_id(2), pl.program_id(3)

    @pl.when((e == 0) & (f == 0))
    def _():
        acc_ref[...] = jnp.zeros_like(acc_ref)

    h = h_ref[0]
    a = _wdot(h, wg_ref[0])
    u = _wdot(h, wu_ref[0])
    act = a * jax.nn.sigmoid(a) * u
    comb = comb_ref[0]
    lane = lax.broadcasted_iota(jnp.int32, comb.shape, 1)
    w_e = jnp.sum(jnp.where(lane == e, comb, 0.0), axis=-1, keepdims=True)
    acc_ref[...] += w_e * _wdot(act, wd_ref[0])

    @pl.when((e == pl.num_programs(2) - 1) & (f == pl.num_programs(3) - 1))
    def _():
        o_ref[0] = x_ref[0] + gate_ref[0] * acc_ref[...]


def _swiglu_residual(x, h, w_gate, w_up, w_down, gate, combine=None):
    bn, t, d = x.shape
    n_exp, _, ff = w_gate.shape
    tm = _row_tile(t)
    tf = 1408
    assert ff % tf == 0
    x_spec = pl.BlockSpec((1, tm, d), lambda b, i, e, f: (b, i, 0))
    wi_spec = pl.BlockSpec((1, d, tf), lambda b, i, e, f: (e, 0, f))
    wd_spec = pl.BlockSpec((1, tf, d), lambda b, i, e, f: (e, f, 0))
    if gate.shape[1] == 1:
        gate_spec = pl.BlockSpec((1, 1, d), lambda b, i, e, f: (b, 0, 0))
    else:
        gate_spec = pl.BlockSpec((1, tm, d), lambda b, i, e, f: (b, i, 0))
    if combine is None:
        kern, in_specs, args = _swiglu_kernel, [x_spec, x_spec], [x, h]
    else:
        kern = _moe_swiglu_kernel
        in_specs = [x_spec, x_spec, pl.BlockSpec((1, tm, LANES), lambda b, i, e, f: (b, i, 0))]
        args = [x, h, combine]
    return pl.pallas_call(
        kern, grid=(bn, t // tm, n_exp, ff // tf),
        in_specs=in_specs + [wi_spec, wi_spec, wd_spec, gate_spec],
        out_specs=x_spec, out_shape=jax.ShapeDtypeStruct((bn, t, d), F32),
        scratch_shapes=[pltpu.VMEM((tm, d), F32)],
        compiler_params=_cparams(("arbitrary",) * 4), name="swiglu",
    )(*args, w_gate, w_up, w_down, gate)


def _s5_prep_kernel(are_ref, aim_ref, ldt_ref, bre_ref, bim_ref,
                    pre_ref, pim_ref, obr_ref, obi_ref, *, seg):
    a_re, a_im = are_ref[0], aim_ref[0]
    dt = jnp.exp(ldt_ref[0])
    steps = lax.broadcasted_iota(jnp.int32, (seg, PACK_STATES), 0).astype(F32) + 1.0
    mag = jnp.exp(steps * (dt * a_re))
    ang = steps * (dt * a_im)
    pre_ref[0] = mag * jnp.cos(ang)
    pim_ref[0] = mag * jnp.sin(ang)
    mag1 = jnp.exp(dt * a_re)
    ab_re = mag1 * jnp.cos(dt * a_im)
    ab_im = mag1 * jnp.sin(dt * a_im)
    den = a_re * a_re + a_im * a_im
    num_re = ab_re - 1.0
    coef_re = (num_re * a_re + ab_im * a_im) / den
    coef_im = (ab_im * a_re - num_re * a_im) / den
    b_re, b_im = bre_ref[0, 0], bim_ref[0, 0]
    obr_ref[0, 0] = coef_re * b_re - coef_im * b_im
    obi_ref[0, 0] = coef_re * b_im + coef_im * b_re


def _blockdiag_b(b):
    nl, g, p, j = b.shape
    b5 = b.reshape(nl, g // PACK_GROUPS, PACK_GROUPS, p, j).transpose(0, 1, 2, 4, 3)
    eye = jnp.eye(PACK_GROUPS, dtype=b.dtype)
    out = b5[:, :, :, :, None, :] * eye[None, None, :, None, :, None]
    return out.reshape(nl, g // PACK_GROUPS, PACK_GROUPS * j, PACK_GROUPS * p)


def _blockdiag_c(c):
    nl, g, j, p = c.shape
    c5 = c.reshape(nl, g // PACK_GROUPS, PACK_GROUPS, j, p).transpose(0, 1, 2, 4, 3)
    eye = jnp.eye(PACK_GROUPS, dtype=c.dtype)
    out = c5[:, :, :, :, None, :] * eye[None, None, :, None, :, None]
    return out.reshape(nl, g // PACK_GROUPS, PACK_GROUPS * p, PACK_GROUPS * j)


def _s5_prep(a_re, a_im, log_dt, b_re, b_im, seg):
    nl, g, p = a_re.shape
    n_state = g * p
    n_pack = g // PACK_GROUPS
    row = lambda a: a.reshape(nl, 1, n_state)
    ldt = jnp.broadcast_to(log_dt[:, :, None], (nl, g, p))
    braw_re, braw_im = _blockdiag_b(b_re), _blockdiag_b(b_im)
    vec_spec = pl.BlockSpec((1, 1, PACK_STATES), lambda l, k: (l, 0, k))
    mat_spec = pl.BlockSpec((1, 1, LANES, PACK_STATES), lambda l, k: (l, k, 0, 0))
    pow_spec = pl.BlockSpec((1, seg, PACK_STATES), lambda l, k: (l, 0, k))
    return pl.pallas_call(
        functools.partial(_s5_prep_kernel, seg=seg),
        grid=(nl, n_pack),
        in_specs=[vec_spec, vec_spec, vec_spec, mat_spec, mat_spec],
        out_specs=[pow_spec, pow_spec, mat_spec, mat_spec],
        out_shape=[jax.ShapeDtypeStruct((nl, seg, n_state), F32)] * 2
        + [jax.ShapeDtypeStruct(braw_re.shape, F32)] * 2,
        compiler_params=_cparams(("arbitrary", "arbitrary")), name="s5_prep",
    )(row(a_re), row(a_im), row(ldt), braw_re, braw_im)


def _s5_prompt_kernel(u_ref, bb_ref, cre_ref, cim_ref, pre_ref, pim_ref, d_ref,
                      g_ref, fre_ref, fim_ref,
                      uperm, sre, sim, gperm, car_re, car_im, ini_re, ini_im, *, seg, chunk):
    n_pack = bb_ref.shape[0]
    n_state = sre.shape[1]

    @pl.when(pl.program_id(1) == 0)
    def _():
        car_re[...] = jnp.zeros_like(car_re)
        car_im[...] = jnp.zeros_like(car_im)

    for pk in range(n_pack):
        for k in range(SUBLANES):
            uperm[pk, pl.ds(k, seg, stride=SUBLANES), :] = (
                u_ref[0, k * seg:(k + 1) * seg, pk * LANES:(pk + 1) * LANES])

    for pk in range(n_pack):
        z = jnp.dot(uperm[pk].astype(BF16), bb_ref[pk], preferred_element_type=F32)
        sre[:, pk * PACK_STATES:(pk + 1) * PACK_STATES] = z[:, :PACK_STATES]
        sim[:, pk * PACK_STATES:(pk + 1) * PACK_STATES] = z[:, PACK_STATES:]

    for c in range(n_state // chunk):
        cs = slice(c * chunk, (c + 1) * chunk)
        lam_re = jnp.broadcast_to(pre_ref[0:1, cs], (SUBLANES, chunk))
        lam_im = jnp.broadcast_to(pim_ref[0:1, cs], (SUBLANES, chunk))

        def scan_step(n, carry, cs=cs, lam_re=lam_re, lam_im=lam_im):
            h_re, h_im = carry
            rows = pl.ds(pl.multiple_of(n * SUBLANES, SUBLANES), SUBLANES)
            n_re = lam_re * h_re - lam_im * h_im + sre[rows, cs]
            n_im = lam_re * h_im + lam_im * h_re + sim[rows, cs]
            sre[rows, cs] = n_re
            sim[rows, cs] = n_im
            return n_re, n_im

        zero = jnp.zeros((SUBLANES, chunk), F32)
        end_re, end_im = lax.fori_loop(0, seg, scan_step, (zero, zero), unroll=4)

        seg_re, seg_im = pre_ref[seg - 1:seg, cs], pim_ref[seg - 1:seg, cs]
        cur_re, cur_im = car_re[:, cs], car_im[:, cs]
        for k in range(SUBLANES):
            ini_re[k:k + 1, cs] = cur_re
            ini_im[k:k + 1, cs] = cur_im
            nxt_re = end_re[k:k + 1] + seg_re * cur_re - seg_im * cur_im
            nxt_im = end_im[k:k + 1] + seg_re * cur_im + seg_im * cur_re
            cur_re, cur_im = nxt_re, nxt_im
        car_re[:, cs] = cur_re
        car_im[:, cs] = cur_im

        in_re, in_im = ini_re[:, cs], ini_im[:, cs]

        def fix_step(n, carry, cs=cs, in_re=in_re, in_im=in_im):
            rows = pl.ds(pl.multiple_of(n * SUBLANES, SUBLANES), SUBLANES)
            p_re, p_im = pre_ref[pl.ds(n, 1), cs], pim_ref[pl.ds(n, 1), cs]
            sre[rows, cs] = sre[rows, cs] + p_re * in_re - p_im * in_im
            sim[rows, cs] = sim[rows, cs] + p_re * in_im + p_im * in_re
            return carry

        lax.fori_loop(0, seg, fix_step, 0, unroll=4)

    fre_ref[0] = car_re[...]
    fim_ref[0] = car_im[...]

    for pk in range(n_pack):
        ss = slice(pk * PACK_STATES, (pk + 1) * PACK_STATES)
        ls = slice(pk * LANES, (pk + 1) * LANES)
        y = (jnp.dot(sre[:, ss].astype(BF16), cre_ref[pk], preferred_element_type=F32)
             - jnp.dot(sim[:, ss].astype(BF16), cim_ref[pk], preferred_element_type=F32))
        y = y + d_ref[:, ls] * uperm[pk]
        gperm[pk] = jax.nn.gelu(y, approximate=True)

    for pk in range(n_pack):
        for k in range(SUBLANES):
            g_ref[0, k * seg:(k + 1) * seg, pk * LANES:(pk + 1) * LANES] = (
                gperm[pk, pl.ds(k, seg, stride=SUBLANES), :].astype(g_ref.dtype))


def _s5_prompt(u, bb16, cre16, cim16, pow_re, pow_im, d_skip):
    bn, t, d = u.shape
    n_pack = bb16.shape[0]
    n_state = pow_re.shape[1]
    tt = S5_TILE
    seg = tt // SUBLANES
    assert t % tt == 0 and pow_re.shape[0] == seg
    const = lambda shape: pl.BlockSpec(shape, lambda b, i: (0,) * len(shape))
    u_spec = pl.BlockSpec((1, tt, d), lambda b, i: (b, i, 0))
    fin_spec = pl.BlockSpec((1, 1, n_state), lambda b, i: (b, 0, 0))
    return pl.pallas_call(
        functools.partial(_s5_prompt_kernel, seg=seg, chunk=512),
        grid=(bn, t // tt),
        in_specs=[u_spec, const(bb16.shape), const(cre16.shape), const(cim16.shape),
                  const(pow_re.shape), const(pow_im.shape), const((1, d))],
        out_specs=[u_spec, fin_spec, fin_spec],
        out_shape=[jax.ShapeDtypeStruct((bn, t, d), BF16),
                   jax.ShapeDtypeStruct((bn, 1, n_state), F32),
                   jax.ShapeDtypeStruct((bn, 1, n_state), F32)],
        scratch_shapes=[pltpu.VMEM((n_pack, tt, LANES), F32), pltpu.VMEM((tt, n_state), F32),
                        pltpu.VMEM((tt, n_state), F32), pltpu.VMEM((n_pack, tt, LANES), F32),
                        pltpu.VMEM((1, n_state), F32), pltpu.VMEM((1, n_state), F32),
                        pltpu.VMEM((SUBLANES, n_state), F32), pltpu.VMEM((SUBLANES, n_state), F32)],
        compiler_params=_cparams(("arbitrary", "arbitrary")), name="s5_prompt",
    )(u, bb16, cre16, cim16, pow_re, pow_im, d_skip.reshape(1, d))


def _s5_sample_kernel(u_ref, hre_ref, him_ref, bre_ref, bim_ref, cre_ref, cim_ref,
                      lre_ref, lim_ref, d_ref, g_ref, sre_ref, sim_ref):
    for pk in range(bre_ref.shape[0]):
        ss = slice(pk * PACK_STATES, (pk + 1) * PACK_STATES)
        ls = slice(pk * LANES, (pk + 1) * LANES)
        u = u_ref[:, ls]
        lam_re, lam_im = lre_ref[:, ss], lim_ref[:, ss]
        h_re, h_im = hre_ref[:, ss], him_ref[:, ss]
        s_re = (jnp.dot(u, bre_ref[pk], precision=HI, preferred_element_type=F32)
                + (lam_re * h_re - lam_im * h_im))
        s_im = (jnp.dot(u, bim_ref[pk], precision=HI, preferred_element_type=F32)
                + (lam_re * h_im + lam_im * h_re))
        sre_ref[:, ss] = s_re
        sim_ref[:, ss] = s_im
        y = (jnp.dot(s_re, cre_ref[pk], precision=HI, preferred_element_type=F32)
             - jnp.dot(s_im, cim_ref[pk], precision=HI, preferred_element_type=F32))
        y = y + d_ref[:, ls] * u
        g_ref[:, ls] = jax.nn.gelu(y, approximate=True).astype(g_ref.dtype)


def _s5_sample(u, h_re, h_im, bbar_re, bbar_im, c_re, c_im, lam_re, lam_im, d_skip):
    rows, d = u.shape
    n_state = h_re.shape[1]
    return pl.pallas_call(
        _s5_sample_kernel,
        out_shape=[jax.ShapeDtypeStruct((rows, d), F32),
                   jax.ShapeDtypeStruct((rows, n_state), F32),
                   jax.ShapeDtypeStruct((rows, n_state), F32)],
        compiler_params=pltpu.CompilerParams(vmem_limit_bytes=VMEM_LIMIT), name="s5_sample",
    )(u, h_re, h_im, bbar_re, bbar_im, c_re, c_im, lam_re, lam_im, d_skip.reshape(1, d))


def _top_blocks_penalty(gate, blk, n_valid_below):
    gate = jnp.where(blk < n_valid_below, gate, NEG_INF)
    n_blk = gate.shape[0]
    sel = jnp.zeros(gate.shape, F32)
    for _ in range(MOBA_TOPK):
        m = jnp.max(gate, axis=0, keepdims=True)
        idx = jnp.min(jnp.where(gate == m, blk, n_blk), axis=0, keepdims=True)
        hit = blk == idx
        sel = jnp.where(hit & (m > NEG_INF), 1.0, sel)
        gate = jnp.where(hit, NEG_INF, gate)
    return jnp.where(sel > 0.0, 0.0, NEG_INF)


def _qkv_prompt_kernel(h_ref, w_ref, k_ref, v_ref, k16_ref, qt_ref, vt_ref, pen_ref, kmean):
    i = pl.program_id(1)
    d = k_ref.shape[-1]
    n_blk = kmean.shape[0]

    @pl.when(i == 0)
    def _():
        kmean[...] = jnp.zeros_like(kmean)

    r = jnp.dot(h_ref[0], w_ref[...], preferred_element_type=F32)
    q, k, v = r[:, :d], r[:, d:2 * d], r[:, 2 * d:]
    k_ref[0] = k
    v_ref[0] = v
    k16_ref[0] = k.astype(BF16)
    q_t = q.T
    qt_ref[0] = (q_t * (LOG2E * HEAD_DIM ** -0.5)).astype(BF16)
    vt_ref[0, 0] = v.T.astype(BF16)

    km = kmean[...]
    blk = lax.broadcasted_iota(jnp.int32, (n_blk, MOBA_BLOCK), 0)
    for h in range(N_HEADS):
        hs = slice(h * HEAD_DIM, (h + 1) * HEAD_DIM)
        gate = jnp.dot(km[:, hs], q_t[hs, :], precision=HI, preferred_element_type=F32)
        pen_ref[0, h] = _top_blocks_penalty(gate, blk, i)

    row = lax.broadcasted_iota(jnp.int32, km.shape, 0)
    kmean[...] = jnp.where(row == i, jnp.mean(k, axis=0, keepdims=True), km)


def _qkv_prompt(h16, w16):
    bn, t, d = h16.shape
    n_blk = t // MOBA_BLOCK
    tile = pl.BlockSpec((1, MOBA_BLOCK, d), lambda b, i: (b, i, 0))
    return pl.pallas_call(
        _qkv_prompt_kernel, grid=(bn, n_blk),
        in_specs=[tile, pl.BlockSpec(w16.shape, lambda b, i: (0, 0))],
        out_specs=[tile, tile, tile,
                   pl.BlockSpec((1, d, MOBA_BLOCK), lambda b, i: (b, 0, i)),
                   pl.BlockSpec((1, 1, d, MOBA_BLOCK), lambda b, i: (b, i, 0, 0)),
                   pl.BlockSpec((1, N_HEADS, n_blk, MOBA_BLOCK), lambda b, i: (b, 0, 0, i))],
        out_shape=[jax.ShapeDtypeStruct((bn, t, d), F32), jax.ShapeDtypeStruct((bn, t, d), F32),
                   jax.ShapeDtypeStruct((bn, t, d), BF16),
                   jax.ShapeDtypeStruct((bn, d, t), BF16),
                   jax.ShapeDtypeStruct((bn, n_blk, d, MOBA_BLOCK), BF16),
                   jax.ShapeDtypeStruct((bn, N_HEADS, n_blk, t), F32)],
        scratch_shapes=[pltpu.VMEM((n_blk, d), F32)],
        compiler_params=_cparams(("arbitrary", "arbitrary")), name="qkv_prompt",
    )(h16, w16)


def _moba_prompt_kernel(slope_ref, qt_ref, k_ref, vt_ref, pen_ref, o_ref):
    hp, i = pl.program_id(1), pl.program_id(2)
    blk = MOBA_BLOCK
    k_iota = lax.broadcasted_iota(jnp.int32, (blk, blk), 0)
    q_iota = lax.broadcasted_iota(jnp.int32, (blk, blk), 1)
    causal = k_iota <= q_iota
    k_local = k_iota.astype(F32)
    head_row = lax.broadcasted_iota(jnp.int32, (2 * HEAD_DIM, blk), 0) // HEAD_DIM
    q_all = qt_ref[0]
    q_h = [jnp.where(head_row == a, q_all, jnp.zeros_like(q_all)) for a in range(2)]
    slope = [slope_ref[2 * hp + a] for a in range(2)]
    k_bias = [slope[a] * k_local for a in range(2)]
    rows = [slice(a * HEAD_DIM, (a + 1) * HEAD_DIM) for a in range(2)]

    def scores(a, n):
        k_n = k_ref[0, pl.ds(pl.multiple_of(n * blk, blk), blk), :]
        return jnp.dot(k_n, q_h[a], preferred_element_type=F32) + k_bias[a]

    def offset(a, n):
        shift = slope[a] * jnp.full((1, blk), (n - i) * blk, jnp.int32).astype(F32)
        return shift + pen_ref[0, a, pl.ds(n, 1), :]

    def weighted_values(a, n, p):
        return jnp.dot(vt_ref[0, n, rows[a], :], p.astype(BF16), preferred_element_type=F32)

    init = []
    for a in range(2):
        s = jnp.where(causal, scores(a, i), NEG_INF)
        m0 = jnp.max(s, axis=0, keepdims=True)
        p = jnp.exp2(s - m0)
        init += [m0, jnp.sum(p, axis=0, keepdims=True), weighted_values(a, i, p)]

    def update(carry, blocks, s_all):
        out = []
        for a in range(2):
            m, l, acc = carry[3 * a:3 * a + 3]
            s_blk = s_all[a]
            off_blk = [offset(a, n) for n in blocks]
            m_new = m
            for s, off in zip(s_blk, off_blk):
                m_new = jnp.maximum(m_new, jnp.max(s, axis=0, keepdims=True) + off)
            alpha = jnp.exp2(m - m_new)
            l = alpha * l
            acc = alpha * acc
            for n, s, off in zip(blocks, s_blk, off_blk):
                p = jnp.exp2(s - (m_new - off))
                l = l + jnp.sum(p, axis=0, keepdims=True)
                acc = acc + weighted_values(a, n, p)
            out += [m_new, l, acc]
        return tuple(out)

    n_pairs = i // 2

    def pair_scores(j):
        blocks = (jnp.minimum(2 * j, i), jnp.minimum(2 * j + 1, i))
        return tuple(scores(a, n) for a in range(2) for n in blocks)

    def pair_step(j, carry):
        s_next = pair_scores(j + 1)
        state = update(carry[:6], (2 * j, 2 * j + 1), (carry[6:8], carry[8:10]))
        return state + s_next

    carry = lax.fori_loop(0, n_pairs, pair_step, tuple(init) + pair_scores(0))[:6]
    carry = lax.fori_loop(2 * n_pairs, i, lambda n, c: update(c, (n,), ((scores(0, n),), (scores(1, n),))),
                          carry)
    outs = [carry[3 * a + 2] / carry[3 * a + 1] for a in range(2)]
    o_ref[0] = jnp.concatenate(outs, axis=0).T.astype(o_ref.dtype)


def _alibi_slopes():
    return 2.0 ** (-8.0 * jnp.arange(1, N_HEADS + 1, dtype=F32) / N_HEADS)


def _moba_prompt(qt16, k16, vt16, pen):
    bn, d, t = qt16.shape
    n_blk = t // MOBA_BLOCK
    pair = 2 * HEAD_DIM
    return pl.pallas_call(
        _moba_prompt_kernel,
        grid_spec=pltpu.PrefetchScalarGridSpec(
            num_scalar_prefetch=0,
            grid=(bn, N_HEADS // 2, n_blk),
            in_specs=[pl.BlockSpec(memory_space=pltpu.SMEM),
                      pl.BlockSpec((1, pair, MOBA_BLOCK), lambda b, hp, i: (b, hp, i)),
                      pl.BlockSpec((1, t, pair), lambda b, hp, i: (b, 0, hp)),
                      pl.BlockSpec((1, n_blk, pair, MOBA_BLOCK), lambda b, hp, i: (b, 0, hp, 0)),
                      pl.BlockSpec((1, 2, n_blk, MOBA_BLOCK), lambda b, hp, i: (b, hp, 0, i))],
            out_specs=pl.BlockSpec((1, MOBA_BLOCK, pair), lambda b, hp, i: (b, i, hp)),
        ),
        out_shape=jax.ShapeDtypeStruct((bn, t, d), BF16),
        compiler_params=_cparams(("arbitrary",) * 3), name="moba_prompt",
    )(_alibi_slopes() * LOG2E, qt16, k16, vt16, pen)


def _matmul_kernel(a_ref, w_ref, o_ref):
    o_ref[...] = _wdot(a_ref[...], w_ref[...])


def _matmul(a, w):
    return pl.pallas_call(
        _matmul_kernel, out_shape=jax.ShapeDtypeStruct((a.shape[0], w.shape[1]), F32),
        compiler_params=pltpu.CompilerParams(vmem_limit_bytes=VMEM_LIMIT), name="matmul",
    )(a, w)


def _moba_sample_kernel(pt_ref, q_ref, kn_ref, vn_ref, slope_ref, k0_ref, k1_ref, v0_ref, v1_ref,
                        o_ref, m_s, l_s, g_s, o_s, *, past_len):
    del pt_ref
    n = pl.program_id(1)
    n_blk = pl.num_programs(1)
    page = k0_ref.shape[2]
    n_heads, hd = q_ref.shape[1:]
    cols = page * n_heads
    assert n_heads & (n_heads - 1) == 0
    q = q_ref[0]
    q_scaled = q * (HEAD_DIM ** -0.5)
    q16 = q_scaled.astype(BF16)
    slope = slope_ref[...]
    col = lax.broadcasted_iota(jnp.int32, (n_heads, cols), 1)
    own_head = (col & (n_heads - 1)) == lax.broadcasted_iota(jnp.int32, (n_heads, cols), 0)
    tok = lax.shift_right_logical(col, n_heads.bit_length() - 1)

    k_sum = jnp.zeros(q.shape, F32)
    s_pages = []
    for half, k_page in enumerate((k0_ref, k1_ref)):
        k3 = k_page[0, 0]
        k_sum = k_sum + jnp.sum(k3, axis=0)
        s = lax.dot_general(q16, k3.reshape(cols, hd).astype(BF16), (((1,), (1,)), ((), ())),
                            preferred_element_type=F32)
        dist = (past_len - (n * MOBA_BLOCK + half * page) - tok).astype(F32)
        s_pages.append(jnp.where(own_head, s - slope * dist, NEG_INF))
    m = jnp.maximum(jnp.max(s_pages[0], axis=-1, keepdims=True),
                    jnp.max(s_pages[1], axis=-1, keepdims=True))
    l = jnp.zeros(m.shape, F32)
    o = jnp.zeros(q.shape, F32)
    for s, v_page in zip(s_pages, (v0_ref, v1_ref)):
        p = jnp.exp(s - m)
        l = l + jnp.sum(p, axis=-1, keepdims=True)
        o = o + jnp.dot(p.astype(BF16), v_page[0, 0].reshape(cols, hd).astype(BF16),
                        preferred_element_type=F32)
    m_s[n] = m
    l_s[n] = l
    o_s[n] = o
    g_s[n] = jnp.sum(q * (k_sum * (1.0 / MOBA_BLOCK)), axis=-1, keepdims=True)

    @pl.when(n == n_blk - 1)
    def _():
        gates = g_s[...]
        blk = lax.broadcasted_iota(jnp.int32, gates.shape, 0)
        sel = jnp.zeros(gates.shape, F32)
        for _ in range(MOBA_TOPK):
            g_max = jnp.max(gates, axis=0, keepdims=True)
            idx = jnp.min(jnp.where(gates == g_max, blk, gates.shape[0]), axis=0, keepdims=True)
            hit = blk == idx
            sel = jnp.where(hit & (g_max > NEG_INF), 1.0, sel)
            gates = jnp.where(hit, NEG_INF, gates)
        chosen = sel > 0.0
        s_own = jnp.sum(q_scaled * kn_ref[0], axis=-1, keepdims=True)
        m_all = m_s[...]
        m_tot = jnp.maximum(jnp.max(jnp.where(chosen, m_all, NEG_INF), axis=0), s_own)
        w = jnp.where(chosen, jnp.exp(m_all - m_tot[None]), 0.0)
        w_own = jnp.exp(s_own - m_tot)
        l_tot = jnp.sum(w * l_s[...], axis=0) + w_own
        o_tot = jnp.sum(w * o_s[...], axis=0) + w_own * vn_ref[0]
        o_ref[0] = o_tot / l_tot


def _moba_sample(q, k_new, v_new, cache_k, cache_v, page_table, layer):
    dec, n_heads, hd = q.shape
    n_pages = page_table.shape[1]
    page = cache_k.shape[2]
    assert MOBA_BLOCK == 2 * page
    n_blk = n_pages // 2
    past_len = n_pages * page
    vec = pl.BlockSpec((1, n_heads, hd), lambda b, n, pt: (b, 0, 0))

    def page_spec(half):
        return pl.BlockSpec((1, 1, page, n_heads, hd),
                            lambda b, n, pt: (layer, pt[b, 2 * n + half], 0, 0, 0))

    stat = pltpu.VMEM((n_blk, n_heads, 1), F32)
    return pl.pallas_call(
        functools.partial(_moba_sample_kernel, past_len=past_len),
        grid_spec=pltpu.PrefetchScalarGridSpec(
            num_scalar_prefetch=1,
            grid=(dec, n_blk),
            in_specs=[vec, vec, vec, pl.BlockSpec((n_heads, 1), lambda b, n, pt: (0, 0)),
                      page_spec(0), page_spec(1), page_spec(0), page_spec(1)],
            out_specs=vec,
            scratch_shapes=[stat, stat, stat, pltpu.VMEM((n_blk, n_heads, hd), F32)],
        ),
        out_shape=jax.ShapeDtypeStruct((dec, n_heads, hd), F32),
        compiler_params=_cparams(("arbitrary", "arbitrary")), name="moba_sample",
    )(page_table, q, k_new, v_new, _alibi_slopes().reshape(n_heads, 1), cache_k, cache_k, cache_v, cache_v)


def _trunk(x, mods, p, past):
    bn, t, d = x.shape
    ssm_re, ssm_im, new_k, new_v = [], [], [], []
    depth = len(mods)
    for layer in range(depth):
        i = layer // 2
        sh1, sc1, g1, sh2, sc2, g2 = mods[layer]
        act_dtype = p['attn_w_o'].dtype
        if layer % 2 == 0:
            u = _modulate(x, p['norm_mix'][layer], sh1, sc1, F32)
            if past is None:
                g, f_re, f_im = _s5_prompt(u, p['s5_bb16'][i], p['s5_cre16'][i], p['s5_cim16'][i],
                                           p['s5_pow_re'][i], p['s5_pow_im'][i], p['ssm_d'][i])
                ssm_re.append(f_re.reshape(bn, -1, SSM_STATE))
                ssm_im.append(f_im.reshape(bn, -1, SSM_STATE))
            else:
                h_re = past[3][i].reshape(t, -1)
                h_im = past[4][i].reshape(t, -1)
                g, s_re, s_im = _s5_sample(u[0], h_re, h_im, p['s5_bbar_re'][i], p['s5_bbar_im'][i],
                                           p['s5_c_re'][i], p['s5_c_im'][i],
                                           p['s5_pow_re'][i][0:1], p['s5_pow_im'][i][0:1], p['ssm_d'][i])
                g = g[None]
                ssm_re.append(s_re.reshape(t, -1, SSM_STATE))
                ssm_im.append(s_im.reshape(t, -1, SSM_STATE))
            x = _proj_residual(x, g, (p['ssm_w_glu_out'][i], p['ssm_w_glu_gate'][i]), g1)
            h = _modulate(x, p['norm_ffn'][layer], sh2, sc2, act_dtype)
            x = _swiglu_residual(x, h, p['ffn_w_gate'][i:i + 1], p['ffn_w_up'][i:i + 1],
                                 p['ffn_w_down'][i:i + 1], g2)
        else:
            h = _modulate(x, p['norm_mix'][layer], sh1, sc1, act_dtype)
            if past is None:
                k, v, k16, qt16, vt16, pen = _qkv_prompt(h, p['attn_w_qkv'][i])
                a = _moba_prompt(qt16, k16, vt16, pen)
                new_k.append(k.reshape(bn, t, N_HEADS, HEAD_DIM))
                new_v.append(v.reshape(bn, t, N_HEADS, HEAD_DIM))
            else:
                qkv = _matmul(h[0], p['attn_w_qkv'][i])
                q, k, v = (qkv[:, j * d:(j + 1) * d].reshape(t, N_HEADS, HEAD_DIM) for j in range(3))
                a = _moba_sample(q, k, v, past[0], past[1], past[2], i).reshape(1, t, d)
                new_k.append(k.reshape(t, 1, N_HEADS, HEAD_DIM))
                new_v.append(v.reshape(t, 1, N_HEADS, HEAD_DIM))
            x = _proj_residual(x, a, (p['attn_w_o'][i],), g1)
            h, comb = _modulate(x, p['norm_ffn'][layer], sh2, sc2, act_dtype,
                                router=(p['moe_w_router'][i], p['moe_b_router'][i]))
            x = _swiglu_residual(x, h, p['moe_w_gate'][i], p['moe_w_up'][i], p['moe_w_down'][i],
                                 g2, combine=comb)
    y = _final_norm(x, p['norm_final'])
    return y, jnp.stack(ssm_re), jnp.stack(ssm_im), jnp.stack(new_k), jnp.stack(new_v)


def kernel(x_prompt, x_sample, cache_k, cache_v, page_table, state_ssm_re, state_ssm_im,
           c_prompt, c_sample, ada_w, ada_b, norm_mix, norm_ffn, norm_final,
           ssm_a_re, ssm_a_im, ssm_log_dt, ssm_b_re, ssm_b_im, ssm_c_re, ssm_c_im, ssm_d,
           ssm_w_glu_out, ssm_w_glu_gate, attn_w_qkv, attn_w_o,
           ffn_w_gate, ffn_w_up, ffn_w_down,
           moe_w_router, moe_b_router, moe_w_gate, moe_w_up, moe_w_down):
    bn, t, d = x_prompt.shape
    dec = x_sample.shape[0]
    depth = ada_w.shape[0]
    assert x_sample.shape[1] == 1 and d == N_HEADS * HEAD_DIM and t % MOBA_BLOCK == 0

    n_cond = bn + dec
    c_all = jnp.pad(jnp.concatenate([c_prompt, c_sample], axis=0), ((0, -n_cond % SUBLANES), (0, 0)))
    mod = _adaln(c_all, ada_w, ada_b)
    mods_p = [[mod[l, :bn, j * d:(j + 1) * d].reshape(bn, 1, d) for j in range(6)] for l in range(depth)]
    mods_s = [[mod[l, bn:n_cond, j * d:(j + 1) * d].reshape(1, dec, d) for j in range(6)] for l in range(depth)]

    pow_re, pow_im, bbar_re, bbar_im = _s5_prep(ssm_a_re, ssm_a_im, ssm_log_dt, ssm_b_re, ssm_b_im,
                                                S5_TILE // SUBLANES)
    c_re_bd, c_im_bd = _blockdiag_c(ssm_c_re), _blockdiag_c(ssm_c_im)
    shared = {
        'norm_mix': norm_mix, 'norm_ffn': norm_ffn, 'norm_final': norm_final, 'ssm_d': ssm_d,
        's5_pow_re': pow_re, 's5_pow_im': pow_im, 's5_bbar_re': bbar_re, 's5_bbar_im': bbar_im,
        's5_bb16': jnp.concatenate([bbar_re, bbar_im], axis=-1).astype(BF16),
        's5_c_re': c_re_bd, 's5_c_im': c_im_bd,
        's5_cre16': c_re_bd.astype(BF16), 's5_cim16': c_im_bd.astype(BF16),
        'moe_w_router': moe_w_router, 'moe_b_router': moe_b_router,
    }
    big = {
        'ssm_w_glu_out': ssm_w_glu_out, 'ssm_w_glu_gate': ssm_w_glu_gate,
        'attn_w_qkv': attn_w_qkv, 'attn_w_o': attn_w_o,
        'ffn_w_gate': ffn_w_gate, 'ffn_w_up': ffn_w_up, 'ffn_w_down': ffn_w_down,
        'moe_w_gate': moe_w_gate, 'moe_w_up': moe_w_up, 'moe_w_down': moe_w_down,
    }
    p_prompt = dict(shared, **{name: w.astype(BF16) for name, w in big.items()})
    p_sample = dict(shared, **big)

    y_p, re_p, im_p, k_p, v_p = _trunk(x_prompt, mods_p, p_prompt, None)

    past = (cache_k, cache_v, page_table, state_ssm_re, state_ssm_im)
    y_s, re_s, im_s, k_s, v_s = _trunk(x_sample.reshape(1, dec, d), mods_s, p_sample, past)
    return (y_p, y_s.reshape(dec, 1, d), re_p, im_p, re_s, im_s, k_p, v_p, k_s, v_s)
```

```python
import functools

import jax
import jax.numpy as jnp
from jax import lax
from jax.experimental import pallas as pl
from jax.experimental.pallas import tpu as pltpu

F32 = jnp.float32
BF16 = jnp.bfloat16
HI = lax.Precision.HIGHEST
NEG_INF = float("-inf")
LOG2E = 1.4426950408889634

SUBLANES = 8
LANES = 128

RMS_EPS = 1e-6
SSM_GROUP = 16
SSM_STATE = 64
PACK_GROUPS = LANES // SSM_GROUP
PACK_STATES = PACK_GROUPS * SSM_STATE
N_HEADS = 16
HEAD_DIM = 64
MOBA_BLOCK = 256
MOBA_TOPK = 3
TOP_K = 2

ROW_TILE = 512
S5_TILE = 256
MOBA_HEADS_PER_STEP = 4
VMEM_LIMIT = 56 * 1024 * 1024


def _cparams(sem):
    return pltpu.CompilerParams(dimension_semantics=sem, vmem_limit_bytes=VMEM_LIMIT)


def _wdot(a, w):
    if w.dtype == F32:
        return jnp.dot(a.astype(F32), w, precision=HI, preferred_element_type=F32)
    return jnp.dot(a.astype(BF16), w, preferred_element_type=F32)


def _row_tile(t):
    return ROW_TILE if t % ROW_TILE == 0 else t


def _mod_spec(arr, tm):
    d = arr.shape[-1]
    if arr.shape[1] == 1:
        return pl.BlockSpec((1, 1, d), lambda b, t: (b, 0, 0))
    return pl.BlockSpec((1, tm, d), lambda b, t: (b, t, 0))


def _adaln_kernel(c_ref, w_ref, b_ref, o_ref):
    c = c_ref[...]
    s = c * jax.nn.sigmoid(c)
    o_ref[0] = jnp.dot(s, w_ref[0], precision=HI, preferred_element_type=F32) + b_ref[0]


def _adaln(c_all, ada_w, ada_b):
    depth, d, n = ada_w.shape
    r = c_all.shape[0]
    tn = 1536
    return pl.pallas_call(
        _adaln_kernel,
        grid=(depth, n // tn),
        in_specs=[pl.BlockSpec((r, d), lambda l, j: (0, 0)),
                  pl.BlockSpec((1, d, tn), lambda l, j: (l, 0, j)),
                  pl.BlockSpec((1, 1, tn), lambda l, j: (l, 0, j))],
        out_specs=pl.BlockSpec((1, r, tn), lambda l, j: (l, 0, j)),
        out_shape=jax.ShapeDtypeStruct((depth, r, n), F32),
        compiler_params=_cparams(("arbitrary", "arbitrary")),
        name="adaln",
    )(c_all, ada_w, ada_b.reshape(depth, 1, n))


def _rms(x, g):
    return x * lax.rsqrt(jnp.mean(x * x, axis=-1, keepdims=True) + RMS_EPS) * g


def _modulate_kernel(x_ref, g_ref, sh_ref, sc_ref, o_ref):
    h = _rms(x_ref[0], g_ref[...]) * (1.0 + sc_ref[0]) + sh_ref[0]
    o_ref[0] = h.astype(o_ref.dtype)


def _modulate_router_kernel(x_ref, g_ref, sh_ref, sc_ref, wr_ref, br_ref, o_ref, comb_ref):
    h = _rms(x_ref[0], g_ref[...]) * (1.0 + sc_ref[0]) + sh_ref[0]
    o_ref[0] = h.astype(o_ref.dtype)
    logits = jnp.dot(h, wr_ref[...], precision=HI, preferred_element_type=F32) + br_ref[...]
    lane = lax.broadcasted_iota(jnp.int32, logits.shape, 1)
    m1 = jnp.max(logits, axis=-1, keepdims=True)
    i1 = jnp.min(jnp.where(logits == m1, lane, LANES), axis=-1, keepdims=True)
    rest = jnp.where(lane == i1, NEG_INF, logits)
    m2 = jnp.max(rest, axis=-1, keepdims=True)
    i2 = jnp.min(jnp.where(rest == m2, lane, LANES), axis=-1, keepdims=True)
    e2 = jnp.exp(m2 - m1)
    den = 1.0 + e2
    comb_ref[0] = jnp.where(lane == i1, 1.0 / den, 0.0) + jnp.where(lane == i2, e2 / den, 0.0)


def _modulate(x, g, shift, scale, out_dtype, router=None):
    bn, t, d = x.shape
    tm = _row_tile(t)
    grid = (bn, t // tm)
    x_spec = pl.BlockSpec((1, tm, d), lambda b, i: (b, i, 0))
    in_specs = [x_spec, pl.BlockSpec((1, d), lambda b, i: (0, 0)), _mod_spec(shift, tm), _mod_spec(scale, tm)]
    args = [x, g.reshape(1, d), shift, scale]
    if router is None:
        return pl.pallas_call(
            _modulate_kernel, grid=grid, in_specs=in_specs, out_specs=x_spec,
            out_shape=jax.ShapeDtypeStruct((bn, t, d), out_dtype),
            compiler_params=_cparams(("arbitrary", "arbitrary")), name="modulate",
        )(*args)
    w_router, b_router = router
    n_exp = w_router.shape[1]
    wr = jnp.pad(w_router, ((0, 0), (0, LANES - n_exp)))
    br = jnp.pad(b_router.reshape(1, n_exp), ((0, 0), (0, LANES - n_exp)), constant_values=NEG_INF)
    in_specs += [pl.BlockSpec((d, LANES), lambda b, i: (0, 0)), pl.BlockSpec((1, LANES), lambda b, i: (0, 0))]
    return pl.pallas_call(
        _modulate_router_kernel, grid=grid, in_specs=in_specs,
        out_specs=[x_spec, pl.BlockSpec((1, tm, LANES), lambda b, i: (b, i, 0))],
        out_shape=[jax.ShapeDtypeStruct((bn, t, d), out_dtype), jax.ShapeDtypeStruct((bn, t, LANES), F32)],
        compiler_params=_cparams(("arbitrary", "arbitrary")), name="modulate_router",
    )(*args, wr, br)


def _final_norm_kernel(x_ref, g_ref, o_ref):
    o_ref[0] = _rms(x_ref[0], g_ref[...])


def _final_norm(x, g):
    bn, t, d = x.shape
    tm = _row_tile(t)
    x_spec = pl.BlockSpec((1, tm, d), lambda b, i: (b, i, 0))
    return pl.pallas_call(
        _final_norm_kernel, grid=(bn, t // tm),
        in_specs=[x_spec, pl.BlockSpec((1, d), lambda b, i: (0, 0))], out_specs=x_spec,
        out_shape=jax.ShapeDtypeStruct((bn, t, d), F32),
        compiler_params=_cparams(("arbitrary", "arbitrary")), name="final_norm",
    )(x, g.reshape(1, d))


def _proj_residual_kernel(x_ref, a_ref, w_ref, gate_ref, o_ref):
    y = _wdot(a_ref[0], w_ref[...])
    o_ref[0] = x_ref[0] + gate_ref[0] * y


def _glu_residual_kernel(x_ref, a_ref, wo_ref, wg_ref, gate_ref, o_ref):
    a = a_ref[0]
    y = _wdot(a, wo_ref[...])
    z = _wdot(a, wg_ref[...])
    o_ref[0] = x_ref[0] + gate_ref[0] * (y * jax.nn.sigmoid(z))


def _proj_residual(x, a, weights, gate):
    bn, t, d = x.shape
    tm = _row_tile(t)
    x_spec = pl.BlockSpec((1, tm, d), lambda b, i: (b, i, 0))
    w_spec = pl.BlockSpec((d, d), lambda b, i: (0, 0))
    kern = _proj_residual_kernel if len(weights) == 1 else _glu_residual_kernel
    return pl.pallas_call(
        kern, grid=(bn, t // tm),
        in_specs=[x_spec, x_spec] + [w_spec] * len(weights) + [_mod_spec(gate, tm)],
        out_specs=x_spec, out_shape=jax.ShapeDtypeStruct((bn, t, d), F32),
        compiler_params=_cparams(("arbitrary", "arbitrary")), name="proj_residual",
    )(x, a, *weights, gate)


def _swiglu_kernel(x_ref, h_ref, wg_ref, wu_ref, wd_ref, gate_ref, o_ref, acc_ref):
    e, f = pl.program_id(2), pl.program_id(3)

    @pl.when((e == 0) & (f == 0))
    def _():
        acc_ref[...] = jnp.zeros_like(acc_ref)

    h = h_ref[0]
    a = _wdot(h, wg_ref[0])
    u = _wdot(h, wu_ref[0])
    acc_ref[...] += _wdot(a * jax.nn.sigmoid(a) * u, wd_ref[0])

    @pl.when((e == pl.num_programs(2) - 1) & (f == pl.num_programs(3) - 1))
    def _():
        o_ref[0] = x_ref[0] + gate_ref[0] * acc_ref[...]


def _moe_swiglu_kernel(x_ref, h_ref, comb_ref, wg_ref, wu_ref, wd_ref, gate_ref, o_ref, acc_ref):
    e, f = pl.program_id(2), pl.program_id(3)

    @pl.when((e == 0) & (f == 0))
    def _():
        acc_ref[...] = jnp.zeros_like(acc_ref)

    h = h_ref[0]
    a = _wdot(h, wg_ref[0])
    u = _wdot(h, wu_ref[0])
    act = a * jax.nn.sigmoid(a) * u
    comb = comb_ref[0]
    lane = lax.broadcasted_iota(jnp.int32, comb.shape, 1)
    w_e = jnp.sum(jnp.where(lane == e, comb, 0.0), axis=-1, keepdims=True)
    acc_ref[...] += w_e * _wdot(act, wd_ref[0])

    @pl.when((e == pl.num_programs(2) - 1) & (f == pl.num_programs(3) - 1))
    def _():
        o_ref[0] = x_ref[0] + gate_ref[0] * acc_ref[...]


def _swiglu_residual(x, h, w_gate, w_up, w_down, gate, combine=None):
    bn, t, d = x.shape
    n_exp, _, ff = w_gate.shape
    tm = _row_tile(t)
    tf = 1408
    assert ff % tf == 0
    x_spec = pl.BlockSpec((1, tm, d), lambda b, i, e, f: (b, i, 0))
    wi_spec = pl.BlockSpec((1, d, tf), lambda b, i, e, f: (e, 0, f))
    wd_spec = pl.BlockSpec((1, tf, d), lambda b, i, e, f: (e, f, 0))
    if gate.shape[1] == 1:
        gate_spec = pl.BlockSpec((1, 1, d), lambda b, i, e, f: (b, 0, 0))
    else:
        gate_spec = pl.BlockSpec((1, tm, d), lambda b, i, e, f: (b, i, 0))
    if combine is None:
        kern, in_specs, args = _swiglu_kernel, [x_spec, x_spec], [x, h]
    else:
        kern = _moe_swiglu_kernel
        in_specs = [x_spec, x_spec, pl.BlockSpec((1, tm, LANES), lambda b, i, e, f: (b, i, 0))]
        args = [x, h, combine]
    return pl.pallas_call(
        kern, grid=(bn, t // tm, n_exp, ff // tf),
        in_specs=in_specs + [wi_spec, wi_spec, wd_spec, gate_spec],
        out_specs=x_spec, out_shape=jax.ShapeDtypeStruct((bn, t, d), F32),
        scratch_shapes=[pltpu.VMEM((tm, d), F32)],
        compiler_params=_cparams(("arbitrary",) * 4), name="swiglu",
    )(*args, w_gate, w_up, w_down, gate)


def _s5_prep_kernel(are_ref, aim_ref, ldt_ref, bre_ref, bim_ref,
                    pre_ref, pim_ref, obr_ref, obi_ref, *, seg):
    a_re, a_im = are_ref[0], aim_ref[0]
    dt = jnp.exp(ldt_ref[0])
    steps = lax.broadcasted_iota(jnp.int32, (seg, PACK_STATES), 0).astype(F32) + 1.0
    mag = jnp.exp(steps * (dt * a_re))
    ang = steps * (dt * a_im)
    pre_ref[0] = mag * jnp.cos(ang)
    pim_ref[0] = mag * jnp.sin(ang)
    mag1 = jnp.exp(dt * a_re)
    ab_re = mag1 * jnp.cos(dt * a_im)
    ab_im = mag1 * jnp.sin(dt * a_im)
    den = a_re * a_re + a_im * a_im
    num_re = ab_re - 1.0
    coef_re = (num_re * a_re + ab_im * a_im) / den
    coef_im = (ab_im * a_re - num_re * a_im) / den
    b_re, b_im = bre_ref[0, 0], bim_ref[0, 0]
    obr_ref[0, 0] = coef_re * b_re - coef_im * b_im
    obi_ref[0, 0] = coef_re * b_im + coef_im * b_re


def _blockdiag_b(b):
    nl, g, p, j = b.shape
    b5 = b.reshape(nl, g // PACK_GROUPS, PACK_GROUPS, p, j).transpose(0, 1, 2, 4, 3)
    eye = jnp.eye(PACK_GROUPS, dtype=b.dtype)
    out = b5[:, :, :, :, None, :] * eye[None, None, :, None, :, None]
    return out.reshape(nl, g // PACK_GROUPS, PACK_GROUPS * j, PACK_GROUPS * p)


def _blockdiag_c(c):
    nl, g, j, p = c.shape
    c5 = c.reshape(nl, g // PACK_GROUPS, PACK_GROUPS, j, p).transpose(0, 1, 2, 4, 3)
    eye = jnp.eye(PACK_GROUPS, dtype=c.dtype)
    out = c5[:, :, :, :, None, :] * eye[None, None, :, None, :, None]
    return out.reshape(nl, g // PACK_GROUPS, PACK_GROUPS * p, PACK_GROUPS * j)


def _s5_prep(a_re, a_im, log_dt, b_re, b_im, seg):
    nl, g, p = a_re.shape
    n_state = g * p
    n_pack = g // PACK_GROUPS
    row = lambda a: a.reshape(nl, 1, n_state)
    ldt = jnp.broadcast_to(log_dt[:, :, None], (nl, g, p))
    braw_re, braw_im = _blockdiag_b(b_re), _blockdiag_b(b_im)
    vec_spec = pl.BlockSpec((1, 1, PACK_STATES), lambda l, k: (l, 0, k))
    mat_spec = pl.BlockSpec((1, 1, LANES, PACK_STATES), lambda l, k: (l, k, 0, 0))
    pow_spec = pl.BlockSpec((1, seg, PACK_STATES), lambda l, k: (l, 0, k))
    return pl.pallas_call(
        functools.partial(_s5_prep_kernel, seg=seg),
        grid=(nl, n_pack),
        in_specs=[vec_spec, vec_spec, vec_spec, mat_spec, mat_spec],
        out_specs=[pow_spec, pow_spec, mat_spec, mat_spec],
        out_shape=[jax.ShapeDtypeStruct((nl, seg, n_state), F32)] * 2
        + [jax.ShapeDtypeStruct(braw_re.shape, F32)] * 2,
        compiler_params=_cparams(("arbitrary", "arbitrary")), name="s5_prep",
    )(row(a_re), row(a_im), row(ldt), braw_re, braw_im)


def _s5_prompt_kernel(u_ref, bb_ref, cre_ref, cim_ref, pre_ref, pim_ref, d_ref,
                      g_ref, fre_ref, fim_ref,
                      uperm, sre, sim, gperm, car_re, car_im, ini_re, ini_im, *, seg, chunk):
    n_pack = bb_ref.shape[0]
    n_state = sre.shape[1]

    @pl.when(pl.program_id(1) == 0)
    def _():
        car_re[...] = jnp.zeros_like(car_re)
        car_im[...] = jnp.zeros_like(car_im)

    for pk in range(n_pack):
        for k in range(SUBLANES):
            uperm[pk, pl.ds(k, seg, stride=SUBLANES), :] = (
                u_ref[0, k * seg:(k + 1) * seg, pk * LANES:(pk + 1) * LANES])

    for pk in range(n_pack):
        z = jnp.dot(uperm[pk].astype(BF16), bb_ref[pk], preferred_element_type=F32)
        sre[:, pk * PACK_STATES:(pk + 1) * PACK_STATES] = z[:, :PACK_STATES]
        sim[:, pk * PACK_STATES:(pk + 1) * PACK_STATES] = z[:, PACK_STATES:]

    for c in range(n_state // chunk):
        cs = slice(c * chunk, (c + 1) * chunk)
        lam_re = jnp.broadcast_to(pre_ref[0:1, cs], (SUBLANES, chunk))
        lam_im = jnp.broadcast_to(pim_ref[0:1, cs], (SUBLANES, chunk))

        def scan_step(n, carry, cs=cs, lam_re=lam_re, lam_im=lam_im):
            h_re, h_im = carry
            rows = pl.ds(pl.multiple_of(n * SUBLANES, SUBLANES), SUBLANES)
            n_re = lam_re * h_re - lam_im * h_im + sre[rows, cs]
            n_im = lam_re * h_im + lam_im * h_re + sim[rows, cs]
            sre[rows, cs] = n_re
            sim[rows, cs] = n_im
            return n_re, n_im

        zero = jnp.zeros((SUBLANES, chunk), F32)
        end_re, end_im = lax.fori_loop(0, seg, scan_step, (zero, zero), unroll=4)

        seg_re, seg_im = pre_ref[seg - 1:seg, cs], pim_ref[seg - 1:seg, cs]
        cur_re, cur_im = car_re[:, cs], car_im[:, cs]
        for k in range(SUBLANES):
            ini_re[k:k + 1, cs] = cur_re
            ini_im[k:k + 1, cs] = cur_im
            nxt_re = end_re[k:k + 1] + seg_re * cur_re - seg_im * cur_im
            nxt_im = end_im[k:k + 1] + seg_re * cur_im + seg_im * cur_re
            cur_re, cur_im = nxt_re, nxt_im
        car_re[:, cs] = cur_re
        car_im[:, cs] = cur_im

        in_re, in_im = ini_re[:, cs], ini_im[:, cs]

        def fix_step(n, carry, cs=cs, in_re=in_re, in_im=in_im):
            rows = pl.ds(pl.multiple_of(n * SUBLANES, SUBLANES), SUBLANES)
            p_re, p_im = pre_ref[pl.ds(n, 1), cs], pim_ref[pl.ds(n, 1), cs]
            sre[rows, cs] = sre[rows, cs] + p_re * in_re - p_im * in_im
            sim[rows, cs] = sim[rows, cs] + p_re * in_im + p_im * in_re
            return carry

        lax.fori_loop(0, seg, fix_step, 0, unroll=4)

    fre_ref[0] = car_re[...]
    fim_ref[0] = car_im[...]

    for pk in range(n_pack):
        ss = slice(pk * PACK_STATES, (pk + 1) * PACK_STATES)
        ls = slice(pk * LANES, (pk + 1) * LANES)
        y = (jnp.dot(sre[:, ss].astype(BF16), cre_ref[pk], preferred_element_type=F32)
             - jnp.dot(sim[:, ss].astype(BF16), cim_ref[pk], preferred_element_type=F32))
        y = y + d_ref[:, ls] * uperm[pk]
        gperm[pk] = jax.nn.gelu(y, approximate=True)

    for pk in range(n_pack):
        for k in range(SUBLANES):
            g_ref[0, k * seg:(k + 1) * seg, pk * LANES:(pk + 1) * LANES] = (
                gperm[pk, pl.ds(k, seg, stride=SUBLANES), :].astype(g_ref.dtype))


def _s5_prompt(u, bb16, cre16, cim16, pow_re, pow_im, d_skip):
    bn, t, d = u.shape
    n_pack = bb16.shape[0]
    n_state = pow_re.shape[1]
    tt = S5_TILE
    seg = tt // SUBLANES
    assert t % tt == 0 and pow_re.shape[0] == seg
    const = lambda shape: pl.BlockSpec(shape, lambda b, i: (0,) * len(shape))
    u_spec = pl.BlockSpec((1, tt, d), lambda b, i: (b, i, 0))
    fin_spec = pl.BlockSpec((1, 1, n_state), lambda b, i: (b, 0, 0))
    return pl.pallas_call(
        functools.partial(_s5_prompt_kernel, seg=seg, chunk=512),
        grid=(bn, t // tt),
        in_specs=[u_spec, const(bb16.shape), const(cre16.shape), const(cim16.shape),
                  const(pow_re.shape), const(pow_im.shape), const((1, d))],
        out_specs=[u_spec, fin_spec, fin_spec],
        out_shape=[jax.ShapeDtypeStruct((bn, t, d), BF16),
                   jax.ShapeDtypeStruct((bn, 1, n_state), F32),
                   jax.ShapeDtypeStruct((bn, 1, n_state), F32)],
        scratch_shapes=[pltpu.VMEM((n_pack, tt, LANES), F32), pltpu.VMEM((tt, n_state), F32),
                        pltpu.VMEM((tt, n_state), F32), pltpu.VMEM((n_pack, tt, LANES), F32),
                        pltpu.VMEM((1, n_state), F32), pltpu.VMEM((1, n_state), F32),
                        pltpu.VMEM((SUBLANES, n_state), F32), pltpu.VMEM((SUBLANES, n_state), F32)],
        compiler_params=_cparams(("arbitrary", "arbitrary")), name="s5_prompt",
    )(u, bb16, cre16, cim16, pow_re, pow_im, d_skip.reshape(1, d))


def _s5_sample_kernel(u_ref, hre_ref, him_ref, bre_ref, bim_ref, cre_ref, cim_ref,
                      lre_ref, lim_ref, d_ref, g_ref, sre_ref, sim_ref):
    for pk in range(bre_ref.shape[0]):
        ss = slice(pk * PACK_STATES, (pk + 1) * PACK_STATES)
        ls = slice(pk * LANES, (pk + 1) * LANES)
        u = u_ref[:, ls]
        lam_re, lam_im = lre_ref[:, ss], lim_ref[:, ss]
        h_re, h_im = hre_ref[:, ss], him_ref[:, ss]
        s_re = (jnp.dot(u, bre_ref[pk], precision=HI, preferred_element_type=F32)
                + (lam_re * h_re - lam_im * h_im))
        s_im = (jnp.dot(u, bim_ref[pk], precision=HI, preferred_element_type=F32)
                + (lam_re * h_im + lam_im * h_re))
        sre_ref[:, ss] = s_re
        sim_ref[:, ss] = s_im
        y = (jnp.dot(s_re, cre_ref[pk], precision=HI, preferred_element_type=F32)
             - jnp.dot(s_im, cim_ref[pk], precision=HI, preferred_element_type=F32))
        y = y + d_ref[:, ls] * u
        g_ref[:, ls] = jax.nn.gelu(y, approximate=True).astype(g_ref.dtype)


def _s5_sample(u, h_re, h_im, bbar_re, bbar_im, c_re, c_im, lam_re, lam_im, d_skip):
    rows, d = u.shape
    n_state = h_re.shape[1]
    return pl.pallas_call(
        _s5_sample_kernel,
        out_shape=[jax.ShapeDtypeStruct((rows, d), F32),
                   jax.ShapeDtypeStruct((rows, n_state), F32),
                   jax.ShapeDtypeStruct((rows, n_state), F32)],
        compiler_params=pltpu.CompilerParams(vmem_limit_bytes=VMEM_LIMIT), name="s5_sample",
    )(u, h_re, h_im, bbar_re, bbar_im, c_re, c_im, lam_re, lam_im, d_skip.reshape(1, d))


def _top_blocks_penalty(gate, blk, n_valid_below):
    gate = jnp.where(blk < n_valid_below, gate, NEG_INF)
    n_blk = gate.shape[0]
    sel = jnp.zeros(gate.shape, F32)
    for _ in range(MOBA_TOPK):
        m = jnp.max(gate, axis=0, keepdims=True)
        idx = jnp.min(jnp.where(gate == m, blk, n_blk), axis=0, keepdims=True)
        hit = blk == idx
        sel = jnp.where(hit & (m > NEG_INF), 1.0, sel)
        gate = jnp.where(hit, NEG_INF, gate)
    return jnp.where(sel > 0.0, 0.0, NEG_INF)


def _qkv_prompt_kernel(h_ref, w_ref, k_ref, v_ref, k16_ref, qt_ref, vt_ref, pen_ref, kmean):
    i = pl.program_id(1)
    d = k_ref.shape[-1]
    n_blk = kmean.shape[0]

    @pl.when(i == 0)
    def _():
        kmean[...] = jnp.zeros_like(kmean)

    r = jnp.dot(h_ref[0], w_ref[...], preferred_element_type=F32)
    q, k, v = r[:, :d], r[:, d:2 * d], r[:, 2 * d:]
    k_ref[0] = k
    v_ref[0] = v
    k16_ref[0] = k.astype(BF16)
    q_t = q.T
    qt_ref[0] = (q_t * (LOG2E * HEAD_DIM ** -0.5)).astype(BF16)
    vt_ref[0, 0] = v.T.astype(BF16)

    km = kmean[...]
    blk = lax.broadcasted_iota(jnp.int32, (n_blk, MOBA_BLOCK), 0)
    for h in range(N_HEADS):
        hs = slice(h * HEAD_DIM, (h + 1) * HEAD_DIM)
        gate = jnp.dot(km[:, hs], q_t[hs, :], precision=HI, preferred_element_type=F32)
        pen_ref[0, h] = _top_blocks_penalty(gate, blk, i)

    row = lax.broadcasted_iota(jnp.int32, km.shape, 0)
    kmean[...] = jnp.where(row == i, jnp.mean(k, axis=0, keepdims=True), km)


def _qkv_prompt(h16, w16):
    bn, t, d = h16.shape
    n_blk = t // MOBA_BLOCK
    tile = pl.BlockSpec((1, MOBA_BLOCK, d), lambda b, i: (b, i, 0))
    return pl.pallas_call(
        _qkv_prompt_kernel, grid=(bn, n_blk),
        in_specs=[tile, pl.BlockSpec(w16.shape, lambda b, i: (0, 0))],
        out_specs=[tile, tile, tile,
                   pl.BlockSpec((1, d, MOBA_BLOCK), lambda b, i: (b, 0, i)),
                   pl.BlockSpec((1, 1, d, MOBA_BLOCK), lambda b, i: (b, i, 0, 0)),
                   pl.BlockSpec((1, N_HEADS, n_blk, MOBA_BLOCK), lambda b, i: (b, 0, 0, i))],
        out_shape=[jax.ShapeDtypeStruct((bn, t, d), F32), jax.ShapeDtypeStruct((bn, t, d), F32),
                   jax.ShapeDtypeStruct((bn, t, d), BF16),
                   jax.ShapeDtypeStruct((bn, d, t), BF16),
                   jax.ShapeDtypeStruct((bn, n_blk, d, MOBA_BLOCK), BF16),
                   jax.ShapeDtypeStruct((bn, N_HEADS, n_blk, t), F32)],
        scratch_shapes=[pltpu.VMEM((n_blk, d), F32)],
        compiler_params=_cparams(("arbitrary", "arbitrary")), name="qkv_prompt",
    )(h16, w16)


def _moba_prompt_kernel(slope_ref, qt_ref, k_ref, vt_ref, pen_ref, o_ref, s_ring, *, n_heads):
    hp, i = pl.program_id(1), pl.program_id(2)
    blk = MOBA_BLOCK
    k_iota = lax.broadcasted_iota(jnp.int32, (blk, blk), 0)
    q_iota = lax.broadcasted_iota(jnp.int32, (blk, blk), 1)
    causal = k_iota <= q_iota
    k_local = k_iota.astype(F32)
    head_row = lax.broadcasted_iota(jnp.int32, (LANES, blk), 0) // HEAD_DIM
    heads = range(n_heads)
    slab = [slice((a // 2) * LANES, (a // 2 + 1) * LANES) for a in heads]
    q_h = []
    for a in heads:
        q_slab = qt_ref[0, slab[a], :]
        q_h.append(jnp.where(head_row == a % 2, q_slab, jnp.zeros_like(q_slab)))
    slope = [slope_ref[n_heads * hp + a] for a in heads]
    k_bias = [slope[a] * k_local for a in heads]
    rows = [slice(a * HEAD_DIM, (a + 1) * HEAD_DIM) for a in heads]

    def scores(a, n):
        k_n = k_ref[0, pl.ds(pl.multiple_of(n * blk, blk), blk), slab[a]]
        return jnp.dot(k_n, q_h[a], preferred_element_type=F32) + k_bias[a]

    def offset(a, n):
        shift = slope[a] * jnp.full((1, blk), (n - i) * blk, jnp.int32).astype(F32)
        return shift + pen_ref[0, a, pl.ds(n, 1), :]

    ones_rows = jnp.ones((2 * SUBLANES, blk), BF16)

    def weighted_values(a, n, p):
        v_aug = jnp.concatenate([vt_ref[0, n, rows[a], :], ones_rows], axis=0)
        return jnp.dot(v_aug, p.astype(BF16), preferred_element_type=F32)

    init = []
    for a in heads:
        s = jnp.where(causal, scores(a, i), NEG_INF)
        m0 = jnp.max(s, axis=0, keepdims=True)
        init += [m0, weighted_values(a, i, jnp.exp2(s - m0))]

    def update(carry, blocks, get_scores):
        out = []
        for a in heads:
            m, acc = carry[2 * a:2 * a + 2]
            off_blk = [offset(a, n) for n in blocks]
            m_new = m
            for k, off in enumerate(off_blk):
                m_new = jnp.maximum(m_new, jnp.max(get_scores(a, k), axis=0, keepdims=True) + off)
            acc = jnp.exp2(m - m_new) * acc
            for k, (n, off) in enumerate(zip(blocks, off_blk)):
                acc = acc + weighted_values(a, n, jnp.exp2(get_scores(a, k) - (m_new - off)))
            out += [m_new, acc]
        return tuple(out)

    n_pairs = i // 2

    def store_pair_scores(slot, j):
        for k, n in enumerate((jnp.minimum(2 * j, i), jnp.minimum(2 * j + 1, i))):
            for a in heads:
                s_ring[slot, 2 * a + k] = scores(a, n)

    def pair_step(j, state):
        slot = j & 1
        state = update(state, (2 * j, 2 * j + 1), lambda a, k: s_ring[slot, 2 * a + k])
        store_pair_scores(1 - slot, j + 1)
        return state

    store_pair_scores(0, 0)
    carry = lax.fori_loop(0, n_pairs, pair_step, tuple(init))

    def single_step(n, state):
        tiles = [scores(a, n) for a in heads]
        return update(state, (n,), lambda a, k: tiles[a])

    carry = lax.fori_loop(2 * n_pairs, i, single_step, carry)
    outs = [carry[2 * a + 1][:HEAD_DIM] / carry[2 * a + 1][HEAD_DIM:HEAD_DIM + 1] for a in heads]
    o_ref[0] = jnp.concatenate(outs, axis=0).T.astype(o_ref.dtype)


def _alibi_slopes():
    return 2.0 ** (-8.0 * jnp.arange(1, N_HEADS + 1, dtype=F32) / N_HEADS)


def _moba_prompt(qt16, k16, vt16, pen):
    bn, d, t = qt16.shape
    n_blk = t // MOBA_BLOCK
    hps = MOBA_HEADS_PER_STEP
    width = hps * HEAD_DIM
    return pl.pallas_call(
        functools.partial(_moba_prompt_kernel, n_heads=hps),
        grid_spec=pltpu.PrefetchScalarGridSpec(
            num_scalar_prefetch=0,
            grid=(bn, N_HEADS // hps, n_blk),
            in_specs=[pl.BlockSpec(memory_space=pltpu.SMEM),
                      pl.BlockSpec((1, width, MOBA_BLOCK), lambda b, hp, i: (b, hp, i)),
                      pl.BlockSpec((1, t, width), lambda b, hp, i: (b, 0, hp)),
                      pl.BlockSpec((1, n_blk, width, MOBA_BLOCK), lambda b, hp, i: (b, 0, hp, 0)),
                      pl.BlockSpec((1, hps, n_blk, MOBA_BLOCK), lambda b, hp, i: (b, hp, 0, i))],
            out_specs=pl.BlockSpec((1, MOBA_BLOCK, width), lambda b, hp, i: (b, i, hp)),
            scratch_shapes=[pltpu.VMEM((2, 2 * hps, MOBA_BLOCK, MOBA_BLOCK), F32)],
        ),
        out_shape=jax.ShapeDtypeStruct((bn, t, d), BF16),
        compiler_params=_cparams(("arbitrary",) * 3), name="moba_prompt",
    )(_alibi_slopes() * LOG2E, qt16, k16, vt16, pen)


def _matmul_kernel(a_ref, w_ref, o_ref):
    o_ref[...] = _wdot(a_ref[...], w_ref[...])


def _matmul(a, w):
    return pl.pallas_call(
        _matmul_kernel, out_shape=jax.ShapeDtypeStruct((a.shape[0], w.shape[1]), F32),
        compiler_params=pltpu.CompilerParams(vmem_limit_bytes=VMEM_LIMIT), name="matmul",
    )(a, w)


def _moba_sample_kernel(pt_ref, q_ref, kn_ref, vn_ref, slope_ref, k0_ref, k1_ref, v0_ref, v1_ref,
                        o_ref, q_lanes, m_s, l_s, g_s, o_s, *, past_len):
    del pt_ref
    n = pl.program_id(1)
    n_blk = pl.num_programs(1)
    page = k0_ref.shape[-1]
    q_col = q_ref[0]

    blk_lane = lax.broadcasted_iota(jnp.int32, m_s.shape, 2)

    @pl.when(n == 0)
    def _():
        q_lanes[...] = jnp.broadcast_to(q_col, q_lanes.shape)
        m_s[...] = jnp.zeros_like(m_s)
        l_s[...] = jnp.zeros_like(l_s)
        g_s[...] = jnp.zeros_like(g_s)
        o_s[...] = jnp.zeros_like(o_s)

    q_b = q_lanes[...]
    slope = slope_ref[...]
    tok = lax.broadcasted_iota(jnp.int32, (1, 1, page), 2)
    raw = [jnp.sum(k_page[0, 0] * q_b, axis=1, keepdims=True) for k_page in (k0_ref, k1_ref)]
    s_pages = []
    for half, r in enumerate(raw):
        dist = (past_len - (n * MOBA_BLOCK + half * page) - tok).astype(F32)
        s_pages.append(r * (HEAD_DIM ** -0.5) - slope * dist)
    m = jnp.maximum(jnp.max(s_pages[0], axis=-1, keepdims=True),
                    jnp.max(s_pages[1], axis=-1, keepdims=True))
    l = jnp.zeros(m.shape, F32)
    pv = jnp.zeros(q_b.shape, F32)
    for s, v_page in zip(s_pages, (v0_ref, v1_ref)):
        p = jnp.exp(s - m)
        l = l + jnp.sum(p, axis=-1, keepdims=True)
        pv = pv + p * v_page[0, 0]
    gate = (jnp.sum(raw[0], axis=-1, keepdims=True)
            + jnp.sum(raw[1], axis=-1, keepdims=True)) * (1.0 / MOBA_BLOCK)
    m_s[...] = jnp.where(blk_lane == n, m, m_s[...])
    l_s[...] = jnp.where(blk_lane == n, l, l_s[...])
    g_s[...] = jnp.where(blk_lane == n, gate, g_s[...])
    blk_lane_o = lax.broadcasted_iota(jnp.int32, o_s.shape, 2)
    o_s[...] = jnp.where(blk_lane_o == n, jnp.sum(pv, axis=-1, keepdims=True), o_s[...])

    @pl.when(n == n_blk - 1)
    def _():
        gates = jnp.where(blk_lane < n_blk, g_s[...], NEG_INF)
        sel = jnp.zeros(gates.shape, F32)
        for _ in range(MOBA_TOPK):
            g_max = jnp.max(gates, axis=-1, keepdims=True)
            idx = jnp.min(jnp.where(gates == g_max, blk_lane, LANES), axis=-1, keepdims=True)
            hit = blk_lane == idx
            sel = jnp.where(hit & (g_max > NEG_INF), 1.0, sel)
            gates = jnp.where(hit, NEG_INF, gates)
        chosen = sel > 0.0
        s_own = jnp.sum(q_col * kn_ref[0], axis=1, keepdims=True) * (HEAD_DIM ** -0.5)
        m_all = m_s[...]
        m_tot = jnp.maximum(jnp.max(jnp.where(chosen, m_all, NEG_INF), axis=-1, keepdims=True), s_own)
        w = jnp.where(chosen, jnp.exp(m_all - m_tot), 0.0)
        w_own = jnp.exp(s_own - m_tot)
        l_tot = jnp.sum(w * l_s[...], axis=-1, keepdims=True) + w_own
        o_tot = jnp.sum(w * o_s[...], axis=-1, keepdims=True) + w_own * vn_ref[0]
        o_ref[0] = o_tot / l_tot


def _moba_sample(q, k_new, v_new, cache_k, cache_v, page_table, layer):
    dec, n_heads, hd, _ = q.shape
    n_pages = page_table.shape[1]
    page = cache_k.shape[-1]
    assert MOBA_BLOCK == 2 * page
    n_blk = n_pages // 2
    past_len = n_pages * page
    vec = pl.BlockSpec((1, n_heads, hd, 1), lambda b, n, pt: (b, 0, 0, 0))

    def page_spec(half):
        return pl.BlockSpec((1, 1, n_heads, hd, page),
                            lambda b, n, pt: (layer, pt[b, 2 * n + half], 0, 0, 0))

    assert n_blk <= LANES
    stat = pltpu.VMEM((n_heads, 1, LANES), F32)
    return pl.pallas_call(
        functools.partial(_moba_sample_kernel, past_len=past_len),
        grid_spec=pltpu.PrefetchScalarGridSpec(
            num_scalar_prefetch=1,
            grid=(dec, n_blk),
            in_specs=[vec, vec, vec, pl.BlockSpec((n_heads, 1, 1), lambda b, n, pt: (0, 0, 0)),
                      page_spec(0), page_spec(1), page_spec(0), page_spec(1)],
            out_specs=vec,
            scratch_shapes=[pltpu.VMEM((n_heads, hd, page), F32), stat, stat, stat,
                            pltpu.VMEM((n_heads, hd, LANES), F32)],
        ),
        out_shape=jax.ShapeDtypeStruct((dec, n_heads, hd, 1), F32),
        compiler_params=_cparams(("arbitrary", "arbitrary")), name="moba_sample",
    )(page_table, q, k_new, v_new, _alibi_slopes().reshape(n_heads, 1, 1), cache_k, cache_k, cache_v, cache_v)


def _trunk(x, mods, p, past):
    bn, t, d = x.shape
    ssm_re, ssm_im, new_k, new_v = [], [], [], []
    depth = len(mods)
    for layer in range(depth):
        i = layer // 2
        sh1, sc1, g1, sh2, sc2, g2 = mods[layer]
        act_dtype = p['attn_w_o'].dtype
        if layer % 2 == 0:
            u = _modulate(x, p['norm_mix'][layer], sh1, sc1, F32)
            if past is None:
                g, f_re, f_im = _s5_prompt(u, p['s5_bb16'][i], p['s5_cre16'][i], p['s5_cim16'][i],
                                           p['s5_pow_re'][i], p['s5_pow_im'][i], p['ssm_d'][i])
                ssm_re.append(f_re.reshape(bn, -1, SSM_STATE))
                ssm_im.append(f_im.reshape(bn, -1, SSM_STATE))
            else:
                h_re = past[3][i].reshape(t, -1)
                h_im = past[4][i].reshape(t, -1)
                g, s_re, s_im = _s5_sample(u[0], h_re, h_im, p['s5_bbar_re'][i], p['s5_bbar_im'][i],
                                           p['s5_c_re'][i], p['s5_c_im'][i],
                                           p['s5_pow_re'][i][0:1], p['s5_pow_im'][i][0:1], p['ssm_d'][i])
                g = g[None]
                ssm_re.append(s_re.reshape(t, -1, SSM_STATE))
                ssm_im.append(s_im.reshape(t, -1, SSM_STATE))
            x = _proj_residual(x, g, (p['ssm_w_glu_out'][i], p['ssm_w_glu_gate'][i]), g1)
            h = _modulate(x, p['norm_ffn'][layer], sh2, sc2, act_dtype)
            x = _swiglu_residual(x, h, p['ffn_w_gate'][i:i + 1], p['ffn_w_up'][i:i + 1],
                                 p['ffn_w_down'][i:i + 1], g2)
        else:
            h = _modulate(x, p['norm_mix'][layer], sh1, sc1, act_dtype)
            if past is None:
                k, v, k16, qt16, vt16, pen = _qkv_prompt(h, p['attn_w_qkv'][i])
                a = _moba_prompt(qt16, k16, vt16, pen)
                new_k.append(k.reshape(bn, t, N_HEADS, HEAD_DIM))
                new_v.append(v.reshape(bn, t, N_HEADS, HEAD_DIM))
            else:
                qkv = _matmul(h[0], p['attn_w_qkv'][i])
                q, k, v = (qkv[:, j * d:(j + 1) * d].reshape(t, N_HEADS, HEAD_DIM, 1) for j in range(3))
                a = _moba_sample(q, k, v, past[0], past[1], past[2], i).reshape(1, t, d)
                new_k.append(k.reshape(t, 1, N_HEADS, HEAD_DIM))
                new_v.append(v.reshape(t, 1, N_HEADS, HEAD_DIM))
            x = _proj_residual(x, a, (p['attn_w_o'][i],), g1)
            h, comb = _modulate(x, p['norm_ffn'][layer], sh2, sc2, act_dtype,
                                router=(p['moe_w_router'][i], p['moe_b_router'][i]))
            x = _swiglu_residual(x, h, p['moe_w_gate'][i], p['moe_w_up'][i], p['moe_w_down'][i],
                                 g2, combine=comb)
    y = _final_norm(x, p['norm_final'])
    return y, jnp.stack(ssm_re), jnp.stack(ssm_im), jnp.stack(new_k), jnp.stack(new_v)


def kernel(x_prompt, x_sample, cache_k, cache_v, page_table, state_ssm_re, state_ssm_im,
           c_prompt, c_sample, ada_w, ada_b, norm_mix, norm_ffn, norm_final,
           ssm_a_re, ssm_a_im, ssm_log_dt, ssm_b_re, ssm_b_im, ssm_c_re, ssm_c_im, ssm_d,
           ssm_w_glu_out, ssm_w_glu_gate, attn_w_qkv, attn_w_o,
           ffn_w_gate, ffn_w_up, ffn_w_down,
           moe_w_router, moe_b_router, moe_w_gate, moe_w_up, moe_w_down):
    bn, t, d = x_prompt.shape
    dec = x_sample.shape[0]
    depth = ada_w.shape[0]
    assert x_sample.shape[1] == 1 and d == N_HEADS * HEAD_DIM and t % MOBA_BLOCK == 0

    n_cond = bn + dec
    c_all = jnp.pad(jnp.concatenate([c_prompt, c_sample], axis=0), ((0, -n_cond % SUBLANES), (0, 0)))
    mod = _adaln(c_all, ada_w, ada_b)
    mods_p = [[mod[l, :bn, j * d:(j + 1) * d].reshape(bn, 1, d) for j in range(6)] for l in range(depth)]
    mods_s = [[mod[l, bn:n_cond, j * d:(j + 1) * d].reshape(1, dec, d) for j in range(6)] for l in range(depth)]

    pow_re, pow_im, bbar_re, bbar_im = _s5_prep(ssm_a_re, ssm_a_im, ssm_log_dt, ssm_b_re, ssm_b_im,
                                                S5_TILE // SUBLANES)
    c_re_bd, c_im_bd = _blockdiag_c(ssm_c_re), _blockdiag_c(ssm_c_im)
    shared = {
        'norm_mix': norm_mix, 'norm_ffn': norm_ffn, 'norm_final': norm_final, 'ssm_d': ssm_d,
        's5_pow_re': pow_re, 's5_pow_im': pow_im, 's5_bbar_re': bbar_re, 's5_bbar_im': bbar_im,
        's5_bb16': jnp.concatenate([bbar_re, bbar_im], axis=-1).astype(BF16),
        's5_c_re': c_re_bd, 's5_c_im': c_im_bd,
        's5_cre16': c_re_bd.astype(BF16), 's5_cim16': c_im_bd.astype(BF16),
        'moe_w_router': moe_w_router, 'moe_b_router': moe_b_router,
    }
    big = {
        'ssm_w_glu_out': ssm_w_glu_out, 'ssm_w_glu_gate': ssm_w_glu_gate,
        'attn_w_qkv': attn_w_qkv, 'attn_w_o': attn_w_o,
        'ffn_w_gate': ffn_w_gate, 'ffn_w_up': ffn_w_up, 'ffn_w_down': ffn_w_down,
        'moe_w_gate': moe_w_gate, 'moe_w_up': moe_w_up, 'moe_w_down': moe_w_down,
    }
    p_prompt = dict(shared, **{name: w.astype(BF16) for name, w in big.items()})
    p_sample = dict(shared, **big)

    y_p, re_p, im_p, k_p, v_p = _trunk(x_prompt, mods_p, p_prompt, None)

    past = (cache_k.transpose(0, 1, 3, 4, 2), cache_v.transpose(0, 1, 3, 4, 2),
            page_table, state_ssm_re, state_ssm_im)
    y_s, re_s, im_s, k_s, v_s = _trunk(x_sample.reshape(1, dec, d), mods_s, p_sample, past)
    return (y_p, y_s.reshape(dec, 1, d), re_p, im_p, re_s, im_s, k_p, v_p, k_s, v_s)
```

```python
import functools

import jax
import jax.numpy as jnp
from jax import lax
from jax.experimental import pallas as pl
from jax.experimental.pallas import tpu as pltpu

F32 = jnp.float32
BF16 = jnp.bfloat16
HI = lax.Precision.HIGHEST
NEG_INF = float("-inf")
LOG2E = 1.4426950408889634

SUBLANES = 8
LANES = 128

RMS_EPS = 1e-6
SSM_GROUP = 16
SSM_STATE = 64
PACK_GROUPS = LANES // SSM_GROUP
PACK_STATES = PACK_GROUPS * SSM_STATE
N_HEADS = 16
HEAD_DIM = 64
MOBA_BLOCK = 256
MOBA_TOPK = 3
TOP_K = 2

ROW_TILE = 512
S5_TILE = 256
MOBA_HEADS_PER_STEP = 8
SAMPLE_BLOCKS_PER_STEP = 2
VMEM_LIMIT = 56 * 1024 * 1024


def _cparams(sem):
    return pltpu.CompilerParams(dimension_semantics=sem, vmem_limit_bytes=VMEM_LIMIT)


def _wdot(a, w):
    if w.dtype == F32:
        return jnp.dot(a.astype(F32), w, precision=HI, preferred_element_type=F32)
    return jnp.dot(a.astype(BF16), w, preferred_element_type=F32)


def _row_tile(t):
    return ROW_TILE if t % ROW_TILE == 0 else t


def _mod_spec(arr, tm):
    d = arr.shape[-1]
    if arr.shape[1] == 1:
        return pl.BlockSpec((1, 1, d), lambda b, t: (b, 0, 0))
    return pl.BlockSpec((1, tm, d), lambda b, t: (b, t, 0))


def _adaln_kernel(c_ref, w_ref, b_ref, o_ref):
    c = c_ref[...]
    s = c * jax.nn.sigmoid(c)
    o_ref[0] = jnp.dot(s, w_ref[0], precision=HI, preferred_element_type=F32) + b_ref[0]


def _adaln(c_all, ada_w, ada_b):
    depth, d, n = ada_w.shape
    r = c_all.shape[0]
    tn = 1536
    return pl.pallas_call(
        _adaln_kernel,
        grid=(depth, n // tn),
        in_specs=[pl.BlockSpec((r, d), lambda l, j: (0, 0)),
                  pl.BlockSpec((1, d, tn), lambda l, j: (l, 0, j)),
                  pl.BlockSpec((1, 1, tn), lambda l, j: (l, 0, j))],
        out_specs=pl.BlockSpec((1, r, tn), lambda l, j: (l, 0, j)),
        out_shape=jax.ShapeDtypeStruct((depth, r, n), F32),
        compiler_params=_cparams(("arbitrary", "arbitrary")),
        name="adaln",
    )(c_all, ada_w, ada_b.reshape(depth, 1, n))


def _rms(x, g):
    return x * lax.rsqrt(jnp.mean(x * x, axis=-1, keepdims=True) + RMS_EPS) * g


def _modulate_kernel(x_ref, g_ref, sh_ref, sc_ref, o_ref):
    h = _rms(x_ref[0], g_ref[...]) * (1.0 + sc_ref[0]) + sh_ref[0]
    o_ref[0] = h.astype(o_ref.dtype)


def _modulate_router_kernel(x_ref, g_ref, sh_ref, sc_ref, wr_ref, br_ref, o_ref, comb_ref):
    h = _rms(x_ref[0], g_ref[...]) * (1.0 + sc_ref[0]) + sh_ref[0]
    o_ref[0] = h.astype(o_ref.dtype)
    logits = jnp.dot(h, wr_ref[...], precision=HI, preferred_element_type=F32) + br_ref[...]
    lane = lax.broadcasted_iota(jnp.int32, logits.shape, 1)
    m1 = jnp.max(logits, axis=-1, keepdims=True)
    i1 = jnp.min(jnp.where(logits == m1, lane, LANES), axis=-1, keepdims=True)
    rest = jnp.where(lane == i1, NEG_INF, logits)
    m2 = jnp.max(rest, axis=-1, keepdims=True)
    i2 = jnp.min(jnp.where(rest == m2, lane, LANES), axis=-1, keepdims=True)
    e2 = jnp.exp(m2 - m1)
    den = 1.0 + e2
    comb_ref[0] = jnp.where(lane == i1, 1.0 / den, 0.0) + jnp.where(lane == i2, e2 / den, 0.0)


def _modulate(x, g, shift, scale, out_dtype, router=None):
    bn, t, d = x.shape
    tm = _row_tile(t)
    grid = (bn, t // tm)
    x_spec = pl.BlockSpec((1, tm, d), lambda b, i: (b, i, 0))
    in_specs = [x_spec, pl.BlockSpec((1, d), lambda b, i: (0, 0)), _mod_spec(shift, tm), _mod_spec(scale, tm)]
    args = [x, g.reshape(1, d), shift, scale]
    if router is None:
        return pl.pallas_call(
            _modulate_kernel, grid=grid, in_specs=in_specs, out_specs=x_spec,
            out_shape=jax.ShapeDtypeStruct((bn, t, d), out_dtype),
            compiler_params=_cparams(("arbitrary", "arbitrary")), name="modulate",
        )(*args)
    w_router, b_router = router
    n_exp = w_router.shape[1]
    wr = jnp.pad(w_router, ((0, 0), (0, LANES - n_exp)))
    br = jnp.pad(b_router.reshape(1, n_exp), ((0, 0), (0, LANES - n_exp)), constant_values=NEG_INF)
    in_specs += [pl.BlockSpec((d, LANES), lambda b, i: (0, 0)), pl.BlockSpec((1, LANES), lambda b, i: (0, 0))]
    return pl.pallas_call(
        _modulate_router_kernel, grid=grid, in_specs=in_specs,
        out_specs=[x_spec, pl.BlockSpec((1, tm, LANES), lambda b, i: (b, i, 0))],
        out_shape=[jax.ShapeDtypeStruct((bn, t, d), out_dtype), jax.ShapeDtypeStruct((bn, t, LANES), F32)],
        compiler_params=_cparams(("arbitrary", "arbitrary")), name="modulate_router",
    )(*args, wr, br)


def _final_norm_kernel(x_ref, g_ref, o_ref):
    o_ref[0] = _rms(x_ref[0], g_ref[...])


def _final_norm(x, g):
    bn, t, d = x.shape
    tm = _row_tile(t)
    x_spec = pl.BlockSpec((1, tm, d), lambda b, i: (b, i, 0))
    return pl.pallas_call(
        _final_norm_kernel, grid=(bn, t // tm),
        in_specs=[x_spec, pl.BlockSpec((1, d), lambda b, i: (0, 0))], out_specs=x_spec,
        out_shape=jax.ShapeDtypeStruct((bn, t, d), F32),
        compiler_params=_cparams(("arbitrary", "arbitrary")), name="final_norm",
    )(x, g.reshape(1, d))


def _proj_residual_kernel(x_ref, a_ref, w_ref, gate_ref, o_ref):
    y = _wdot(a_ref[0], w_ref[0])
    o_ref[0] = x_ref[0] + gate_ref[0] * y


def _glu_residual_kernel(x_ref, a_ref, wo_ref, wg_ref, gate_ref, o_ref):
    a = a_ref[0]
    y = _wdot(a, wo_ref[0])
    z = _wdot(a, wg_ref[0])
    o_ref[0] = x_ref[0] + gate_ref[0] * (y * jax.nn.sigmoid(z))


def _proj_residual(x, a, weights, gate, layer):
    bn, t, d = x.shape
    tm = _row_tile(t)
    x_spec = pl.BlockSpec((1, tm, d), lambda b, i: (b, i, 0))
    w_spec = pl.BlockSpec((1, d, d), lambda b, i: (layer, 0, 0))
    kern = _proj_residual_kernel if len(weights) == 1 else _glu_residual_kernel
    return pl.pallas_call(
        kern, grid=(bn, t // tm),
        in_specs=[x_spec, x_spec] + [w_spec] * len(weights) + [_mod_spec(gate, tm)],
        out_specs=x_spec, out_shape=jax.ShapeDtypeStruct((bn, t, d), F32),
        compiler_params=_cparams(("arbitrary", "arbitrary")), name="proj_residual",
    )(x, a, *weights, gate)


def _swiglu_kernel(x_ref, h_ref, wg_ref, wu_ref, wd_ref, gate_ref, o_ref, acc_ref):
    e, f = pl.program_id(2), pl.program_id(3)

    @pl.when((e == 0) & (f == 0))
    def _():
        acc_ref[...] = jnp.zeros_like(acc_ref)

    h = h_ref[0]
    a = _wdot(h, wg_ref[0, 0])
    u = _wdot(h, wu_ref[0, 0])
    acc_ref[...] += _wdot(a * jax.nn.sigmoid(a) * u, wd_ref[0, 0])

    @pl.when((e == pl.num_programs(2) - 1) & (f == pl.num_programs(3) - 1))
    def _():
        o_ref[0] = x_ref[0] + gate_ref[0] * acc_ref[...]


def _moe_swiglu_kernel(x_ref, h_ref, comb_ref, wg_ref, wu_ref, wd_ref, gate_ref, o_ref, acc_ref):
    e, f = pl.program_id(2), pl.program_id(3)

    @pl.when((e == 0) & (f == 0))
    def _():
        acc_ref[...] = jnp.zeros_like(acc_ref)

    h = h_ref[0]
    a = _wdot(h, wg_ref[0, 0])
    u = _wdot(h, wu_ref[0, 0])
    act = a * jax.nn.sigmoid(a) * u
    comb = comb_ref[0]
    lane = lax.broadcasted_iota(jnp.int32, comb.shape, 1)
    w_e = jnp.sum(jnp.where(lane == e, comb, 0.0), axis=-1, keepdims=True)
    acc_ref[...] += w_e * _wdot(act, wd_ref[0, 0])

    @pl.when((e == pl.num_programs(2) - 1) & (f == pl.num_programs(3) - 1))
    def _():
        o_ref[0] = x_ref[0] + gate_ref[0] * acc_ref[...]


def _swiglu_residual(x, h, w_gate, w_up, w_down, gate, layer, combine=None):
    bn, t, d = x.shape
    _, n_exp, _, ff = w_gate.shape
    tm = _row_tile(t)
    tf = 1408
    assert ff % tf == 0
    x_spec = pl.BlockSpec((1, tm, d), lambda b, i, e, f: (b, i, 0))
    wi_spec = pl.BlockSpec((1, 1, d, tf), lambda b, i, e, f: (layer, e, 0, f))
    wd_spec = pl.BlockSpec((1, 1, tf, d), lambda b, i, e, f: (layer, e, f, 0))
    if gate.shape[1] == 1:
        gate_spec = pl.BlockSpec((1, 1, d), lambda b, i, e, f: (b, 0, 0))
    else:
        gate_spec = pl.BlockSpec((1, tm, d), lambda b, i, e, f: (b, i, 0))
    if combine is None:
        kern, in_specs, args = _swiglu_kernel, [x_spec, x_spec], [x, h]
    else:
        kern = _moe_swiglu_kernel
        in_specs = [x_spec, x_spec, pl.BlockSpec((1, tm, LANES), lambda b, i, e, f: (b, i, 0))]
        args = [x, h, combine]
    return pl.pallas_call(
        kern, grid=(bn, t // tm, n_exp, ff // tf),
        in_specs=in_specs + [wi_spec, wi_spec, wd_spec, gate_spec],
        out_specs=x_spec, out_shape=jax.ShapeDtypeStruct((bn, t, d), F32),
        scratch_shapes=[pltpu.VMEM((tm, d), F32)],
        compiler_params=_cparams(("arbitrary",) * 4), name="swiglu",
    )(*args, w_gate, w_up, w_down, gate)


def _s5_prep_kernel(are_ref, aim_ref, ldt_ref, bre_ref, bim_ref,
                    pre_ref, pim_ref, obr_ref, obi_ref, *, seg):
    a_re, a_im = are_ref[0], aim_ref[0]
    dt = jnp.exp(ldt_ref[0])
    steps = lax.broadcasted_iota(jnp.int32, (seg, PACK_STATES), 0).astype(F32) + 1.0
    mag = jnp.exp(steps * (dt * a_re))
    ang = steps * (dt * a_im)
    pre_ref[0] = mag * jnp.cos(ang)
    pim_ref[0] = mag * jnp.sin(ang)
    mag1 = jnp.exp(dt * a_re)
    ab_re = mag1 * jnp.cos(dt * a_im)
    ab_im = mag1 * jnp.sin(dt * a_im)
    den = a_re * a_re + a_im * a_im
    num_re = ab_re - 1.0
    coef_re = (num_re * a_re + ab_im * a_im) / den
    coef_im = (ab_im * a_re - num_re * a_im) / den
    b_re, b_im = bre_ref[0, 0], bim_ref[0, 0]
    obr_ref[0, 0] = coef_re * b_re - coef_im * b_im
    obi_ref[0, 0] = coef_re * b_im + coef_im * b_re


def _blockdiag_b(b):
    nl, g, p, j = b.shape
    b5 = b.reshape(nl, g // PACK_GROUPS, PACK_GROUPS, p, j).transpose(0, 1, 2, 4, 3)
    eye = jnp.eye(PACK_GROUPS, dtype=b.dtype)
    out = b5[:, :, :, :, None, :] * eye[None, None, :, None, :, None]
    return out.reshape(nl, g // PACK_GROUPS, PACK_GROUPS * j, PACK_GROUPS * p)


def _blockdiag_c(c):
    nl, g, j, p = c.shape
    c5 = c.reshape(nl, g // PACK_GROUPS, PACK_GROUPS, j, p).transpose(0, 1, 2, 4, 3)
    eye = jnp.eye(PACK_GROUPS, dtype=c.dtype)
    out = c5[:, :, :, :, None, :] * eye[None, None, :, None, :, None]
    return out.reshape(nl, g // PACK_GROUPS, PACK_GROUPS * p, PACK_GROUPS * j)


def _s5_prep(a_re, a_im, log_dt, b_re, b_im, seg):
    nl, g, p = a_re.shape
    n_state = g * p
    n_pack = g // PACK_GROUPS
    row = lambda a: a.reshape(nl, 1, n_state)
    ldt = jnp.broadcast_to(log_dt[:, :, None], (nl, g, p))
    braw_re, braw_im = _blockdiag_b(b_re), _blockdiag_b(b_im)
    vec_spec = pl.BlockSpec((1, 1, PACK_STATES), lambda l, k: (l, 0, k))
    mat_spec = pl.BlockSpec((1, 1, LANES, PACK_STATES), lambda l, k: (l, k, 0, 0))
    pow_spec = pl.BlockSpec((1, seg, PACK_STATES), lambda l, k: (l, 0, k))
    return pl.pallas_call(
        functools.partial(_s5_prep_kernel, seg=seg),
        grid=(nl, n_pack),
        in_specs=[vec_spec, vec_spec, vec_spec, mat_spec, mat_spec],
        out_specs=[pow_spec, pow_spec, mat_spec, mat_spec],
        out_shape=[jax.ShapeDtypeStruct((nl, seg, n_state), F32)] * 2
        + [jax.ShapeDtypeStruct(braw_re.shape, F32)] * 2,
        compiler_params=_cparams(("arbitrary", "arbitrary")), name="s5_prep",
    )(row(a_re), row(a_im), row(ldt), braw_re, braw_im)


def _s5_prompt_kernel(u_ref, bb_ref, cre_ref, cim_ref, pre_ref, pim_ref, d_ref,
                      g_ref, fre_ref, fim_ref,
                      uperm, sre, sim, gperm, car_re, car_im, ini_re, ini_im, *, seg, chunk):
    n_pack = bb_ref.shape[0]
    n_state = sre.shape[1]

    @pl.when(pl.program_id(1) == 0)
    def _():
        car_re[...] = jnp.zeros_like(car_re)
        car_im[...] = jnp.zeros_like(car_im)

    for pk in range(n_pack):
        for k in range(SUBLANES):
            uperm[pk, pl.ds(k, seg, stride=SUBLANES), :] = (
                u_ref[0, k * seg:(k + 1) * seg, pk * LANES:(pk + 1) * LANES])

    for pk in range(n_pack):
        z = jnp.dot(uperm[pk].astype(BF16), bb_ref[pk], preferred_element_type=F32)
        sre[:, pk * PACK_STATES:(pk + 1) * PACK_STATES] = z[:, :PACK_STATES]
        sim[:, pk * PACK_STATES:(pk + 1) * PACK_STATES] = z[:, PACK_STATES:]

    for c in range(n_state // chunk):
        cs = slice(c * chunk, (c + 1) * chunk)
        lam_re = jnp.broadcast_to(pre_ref[0:1, cs], (SUBLANES, chunk))
        lam_im = jnp.broadcast_to(pim_ref[0:1, cs], (SUBLANES, chunk))

        def scan_step(n, carry, cs=cs, lam_re=lam_re, lam_im=lam_im):
            h_re, h_im = carry
            rows = pl.ds(pl.multiple_of(n * SUBLANES, SUBLANES), SUBLANES)
            n_re = lam_re * h_re - lam_im * h_im + sre[rows, cs]
            n_im = lam_re * h_im + lam_im * h_re + sim[rows, cs]
            sre[rows, cs] = n_re
            sim[rows, cs] = n_im
            return n_re, n_im

        zero = jnp.zeros((SUBLANES, chunk), F32)
        end_re, end_im = lax.fori_loop(0, seg, scan_step, (zero, zero), unroll=4)

        seg_re, seg_im = pre_ref[seg - 1:seg, cs], pim_ref[seg - 1:seg, cs]
        cur_re, cur_im = car_re[:, cs], car_im[:, cs]
        for k in range(SUBLANES):
            ini_re[k:k + 1, cs] = cur_re
            ini_im[k:k + 1, cs] = cur_im
            nxt_re = end_re[k:k + 1] + seg_re * cur_re - seg_im * cur_im
            nxt_im = end_im[k:k + 1] + seg_re * cur_im + seg_im * cur_re
            cur_re, cur_im = nxt_re, nxt_im
        car_re[:, cs] = cur_re
        car_im[:, cs] = cur_im

        in_re, in_im = ini_re[:, cs], ini_im[:, cs]

        def fix_step(n, carry, cs=cs, in_re=in_re, in_im=in_im):
            rows = pl.ds(pl.multiple_of(n * SUBLANES, SUBLANES), SUBLANES)
            p_re, p_im = pre_ref[pl.ds(n, 1), cs], pim_ref[pl.ds(n, 1), cs]
            sre[rows, cs] = sre[rows, cs] + p_re * in_re - p_im * in_im
            sim[rows, cs] = sim[rows, cs] + p_re * in_im + p_im * in_re
            return carry

        lax.fori_loop(0, seg, fix_step, 0, unroll=4)

    fre_ref[0] = car_re[...]
    fim_ref[0] = car_im[...]

    for pk in range(n_pack):
        ss = slice(pk * PACK_STATES, (pk + 1) * PACK_STATES)
        ls = slice(pk * LANES, (pk + 1) * LANES)
        y = (jnp.dot(sre[:, ss].astype(BF16), cre_ref[pk], preferred_element_type=F32)
             - jnp.dot(sim[:, ss].astype(BF16), cim_ref[pk], preferred_element_type=F32))
        y = y + d_ref[:, ls] * uperm[pk]
        gperm[pk] = jax.nn.gelu(y, approximate=True)

    for pk in range(n_pack):
        for k in range(SUBLANES):
            g_ref[0, k * seg:(k + 1) * seg, pk * LANES:(pk + 1) * LANES] = (
                gperm[pk, pl.ds(k, seg, stride=SUBLANES), :].astype(g_ref.dtype))


def _s5_prompt(u, bb16, cre16, cim16, pow_re, pow_im, d_skip):
    bn, t, d = u.shape
    n_pack = bb16.shape[0]
    n_state = pow_re.shape[1]
    tt = S5_TILE
    seg = tt // SUBLANES
    assert t % tt == 0 and pow_re.shape[0] == seg
    const = lambda shape: pl.BlockSpec(shape, lambda b, i: (0,) * len(shape))
    u_spec = pl.BlockSpec((1, tt, d), lambda b, i: (b, i, 0))
    fin_spec = pl.BlockSpec((1, 1, n_state), lambda b, i: (b, 0, 0))
    return pl.pallas_call(
        functools.partial(_s5_prompt_kernel, seg=seg, chunk=512),
        grid=(bn, t // tt),
        in_specs=[u_spec, const(bb16.shape), const(cre16.shape), const(cim16.shape),
                  const(pow_re.shape), const(pow_im.shape), const((1, d))],
        out_specs=[u_spec, fin_spec, fin_spec],
        out_shape=[jax.ShapeDtypeStruct((bn, t, d), BF16),
                   jax.ShapeDtypeStruct((bn, 1, n_state), F32),
                   jax.ShapeDtypeStruct((bn, 1, n_state), F32)],
        scratch_shapes=[pltpu.VMEM((n_pack, tt, LANES), F32), pltpu.VMEM((tt, n_state), F32),
                        pltpu.VMEM((tt, n_state), F32), pltpu.VMEM((n_pack, tt, LANES), F32),
                        pltpu.VMEM((1, n_state), F32), pltpu.VMEM((1, n_state), F32),
                        pltpu.VMEM((SUBLANES, n_state), F32), pltpu.VMEM((SUBLANES, n_state), F32)],
        compiler_params=_cparams(("arbitrary", "arbitrary")), name="s5_prompt",
    )(u, bb16, cre16, cim16, pow_re, pow_im, d_skip.reshape(1, d))


def _s5_sample_kernel(u_ref, hre_ref, him_ref, bre_ref, bim_ref, cre_ref, cim_ref,
                      lre_ref, lim_ref, d_ref, g_ref, sre_ref, sim_ref):
    for pk in range(bre_ref.shape[0]):
        ss = slice(pk * PACK_STATES, (pk + 1) * PACK_STATES)
        ls = slice(pk * LANES, (pk + 1) * LANES)
        u = u_ref[:, ls]
        lam_re, lam_im = lre_ref[:, ss], lim_ref[:, ss]
        h_re, h_im = hre_ref[:, ss], him_ref[:, ss]
        s_re = (jnp.dot(u, bre_ref[pk], precision=HI, preferred_element_type=F32)
                + (lam_re * h_re - lam_im * h_im))
        s_im = (jnp.dot(u, bim_ref[pk], precision=HI, preferred_element_type=F32)
                + (lam_re * h_im + lam_im * h_re))
        sre_ref[:, ss] = s_re
        sim_ref[:, ss] = s_im
        y = (jnp.dot(s_re, cre_ref[pk], precision=HI, preferred_element_type=F32)
             - jnp.dot(s_im, cim_ref[pk], precision=HI, preferred_element_type=F32))
        y = y + d_ref[:, ls] * u
        g_ref[:, ls] = jax.nn.gelu(y, approximate=True).astype(g_ref.dtype)


def _s5_sample(u, h_re, h_im, bbar_re, bbar_im, c_re, c_im, lam_re, lam_im, d_skip):
    rows, d = u.shape
    n_state = h_re.shape[1]
    return pl.pallas_call(
        _s5_sample_kernel,
        out_shape=[jax.ShapeDtypeStruct((rows, d), F32),
                   jax.ShapeDtypeStruct((rows, n_state), F32),
                   jax.ShapeDtypeStruct((rows, n_state), F32)],
        compiler_params=pltpu.CompilerParams(vmem_limit_bytes=VMEM_LIMIT), name="s5_sample",
    )(u, h_re, h_im, bbar_re, bbar_im, c_re, c_im, lam_re, lam_im, d_skip.reshape(1, d))


def _top_blocks_penalty(gate, blk, n_valid_below):
    gate = jnp.where(blk < n_valid_below, gate, NEG_INF)
    n_blk = gate.shape[0]
    sel = jnp.zeros(gate.shape, F32)
    for _ in range(MOBA_TOPK):
        m = jnp.max(gate, axis=0, keepdims=True)
        idx = jnp.min(jnp.where(gate == m, blk, n_blk), axis=0, keepdims=True)
        hit = blk == idx
        sel = jnp.where(hit & (m > NEG_INF), 1.0, sel)
        gate = jnp.where(hit, NEG_INF, gate)
    return jnp.where(sel > 0.0, 0.0, NEG_INF)


def _qkv_prompt_kernel(h_ref, w_ref, kt_ref, vt_ref, k16_ref, qt16_ref, vt16_ref, pen_ref, kmean):
    i = pl.program_id(1)
    d = k16_ref.shape[-1]
    n_blk = kmean.shape[0]

    @pl.when(i == 0)
    def _():
        kmean[...] = jnp.zeros_like(kmean)

    r = jnp.dot(h_ref[0], w_ref[0], preferred_element_type=F32)
    q, k, v = r[:, :d], r[:, d:2 * d], r[:, 2 * d:]
    k16_ref[0] = k.astype(BF16)
    q_t, v_t = q.T, v.T
    kt_ref[0] = k.T
    vt_ref[0] = v_t
    qt16_ref[0] = (q_t * (LOG2E * HEAD_DIM ** -0.5)).astype(BF16)
    vt16_ref[0, 0] = v_t.astype(BF16)

    km = kmean[...]
    blk = lax.broadcasted_iota(jnp.int32, (n_blk, MOBA_BLOCK), 0)
    for h in range(N_HEADS):
        hs = slice(h * HEAD_DIM, (h + 1) * HEAD_DIM)
        gate = jnp.dot(km[:, hs], q_t[hs, :], precision=HI, preferred_element_type=F32)
        pen_ref[0, h] = _top_blocks_penalty(gate, blk, i)

    row = lax.broadcasted_iota(jnp.int32, km.shape, 0)
    kmean[...] = jnp.where(row == i, jnp.mean(k, axis=0, keepdims=True), km)


def _qkv_prompt(h16, w16, layer):
    bn, t, d = h16.shape
    n_blk = t // MOBA_BLOCK
    tile = pl.BlockSpec((1, MOBA_BLOCK, d), lambda b, i: (b, i, 0))
    tile_t = pl.BlockSpec((1, d, MOBA_BLOCK), lambda b, i: (b, 0, i))
    return pl.pallas_call(
        _qkv_prompt_kernel, grid=(bn, n_blk),
        in_specs=[tile, pl.BlockSpec((1,) + w16.shape[1:], lambda b, i: (layer, 0, 0))],
        out_specs=[tile_t, tile_t, tile, tile_t,
                   pl.BlockSpec((1, 1, d, MOBA_BLOCK), lambda b, i: (b, i, 0, 0)),
                   pl.BlockSpec((1, N_HEADS, n_blk, MOBA_BLOCK), lambda b, i: (b, 0, 0, i))],
        out_shape=[jax.ShapeDtypeStruct((bn, d, t), F32), jax.ShapeDtypeStruct((bn, d, t), F32),
                   jax.ShapeDtypeStruct((bn, t, d), BF16),
                   jax.ShapeDtypeStruct((bn, d, t), BF16),
                   jax.ShapeDtypeStruct((bn, n_blk, d, MOBA_BLOCK), BF16),
                   jax.ShapeDtypeStruct((bn, N_HEADS, n_blk, t), F32)],
        scratch_shapes=[pltpu.VMEM((n_blk, d), F32)],
        compiler_params=_cparams(("arbitrary", "arbitrary")), name="qkv_prompt",
    )(h16, w16)


def _moba_prompt_kernel(slope_ref, qt_ref, k_ref, vt_ref, pen_ref, o_ref, s_ring, *, n_heads):
    hp, i = pl.program_id(1), pl.program_id(2)
    blk = MOBA_BLOCK
    k_iota = lax.broadcasted_iota(jnp.int32, (blk, blk), 0)
    q_iota = lax.broadcasted_iota(jnp.int32, (blk, blk), 1)
    causal = k_iota <= q_iota
    k_local = k_iota.astype(F32)
    head_row = lax.broadcasted_iota(jnp.int32, (LANES, blk), 0) // HEAD_DIM
    heads = range(n_heads)
    slab = [slice((a // 2) * LANES, (a // 2 + 1) * LANES) for a in heads]
    q_h = []
    for a in heads:
        q_slab = qt_ref[0, slab[a], :]
        q_h.append(jnp.where(head_row == a % 2, q_slab, jnp.zeros_like(q_slab)))
    slope = [slope_ref[n_heads * hp + a] for a in heads]
    k_bias = [slope[a] * k_local for a in heads]
    rows = [slice(a * HEAD_DIM, (a + 1) * HEAD_DIM) for a in heads]

    def scores(a, n):
        k_n = k_ref[0, pl.ds(pl.multiple_of(n * blk, blk), blk), slab[a]]
        return jnp.dot(k_n, q_h[a], preferred_element_type=F32) + k_bias[a]

    def offset(a, n):
        shift = slope[a] * jnp.full((1, blk), (n - i) * blk, jnp.int32).astype(F32)
        return shift + pen_ref[0, a, pl.ds(n, 1), :]

    ones_rows = jnp.ones((2 * SUBLANES, blk), BF16)

    def weighted_values(a, n, p):
        v_aug = jnp.concatenate([vt_ref[0, n, rows[a], :], ones_rows], axis=0)
        return jnp.dot(v_aug, p.astype(BF16), preferred_element_type=F32)

    init = []
    for a in heads:
        s = jnp.where(causal, scores(a, i), NEG_INF)
        m0 = jnp.max(s, axis=0, keepdims=True)
        init += [m0, weighted_values(a, i, jnp.exp2(s - m0))]

    def update(carry, blocks, get_scores):
        out = []
        for a in heads:
            m, acc = carry[2 * a:2 * a + 2]
            off_blk = [offset(a, n) for n in blocks]
            m_new = m
            for k, off in enumerate(off_blk):
                m_new = jnp.maximum(m_new, jnp.max(get_scores(a, k), axis=0, keepdims=True) + off)
            acc = jnp.exp2(m - m_new) * acc
            for k, (n, off) in enumerate(zip(blocks, off_blk)):
                acc = acc + weighted_values(a, n, jnp.exp2(get_scores(a, k) - (m_new - off)))
            out += [m_new, acc]
        return tuple(out)

    n_pairs = i // 2

    def store_pair_scores(slot, j):
        for k, n in enumerate((jnp.minimum(2 * j, i), jnp.minimum(2 * j + 1, i))):
            for a in heads:
                s_ring[slot, 2 * a + k] = scores(a, n)

    def pair_step(j, state):
        slot = j & 1
        state = update(state, (2 * j, 2 * j + 1), lambda a, k: s_ring[slot, 2 * a + k])
        store_pair_scores(1 - slot, j + 1)
        return state

    store_pair_scores(0, 0)
    carry = lax.fori_loop(0, n_pairs, pair_step, tuple(init))

    def single_step(n, state):
        tiles = [scores(a, n) for a in heads]
        return update(state, (n,), lambda a, k: tiles[a])

    carry = lax.fori_loop(2 * n_pairs, i, single_step, carry)
    outs = [carry[2 * a + 1][:HEAD_DIM] / carry[2 * a + 1][HEAD_DIM:HEAD_DIM + 1] for a in heads]
    o_ref[0] = jnp.concatenate(outs, axis=0).T.astype(o_ref.dtype)


def _alibi_slopes():
    return 2.0 ** (-8.0 * jnp.arange(1, N_HEADS + 1, dtype=F32) / N_HEADS)


def _moba_prompt(qt16, k16, vt16, pen):
    bn, d, t = qt16.shape
    n_blk = t // MOBA_BLOCK
    hps = MOBA_HEADS_PER_STEP
    width = hps * HEAD_DIM
    return pl.pallas_call(
        functools.partial(_moba_prompt_kernel, n_heads=hps),
        grid_spec=pltpu.PrefetchScalarGridSpec(
            num_scalar_prefetch=0,
            grid=(bn, N_HEADS // hps, n_blk),
            in_specs=[pl.BlockSpec(memory_space=pltpu.SMEM),
                      pl.BlockSpec((1, width, MOBA_BLOCK), lambda b, hp, i: (b, hp, i)),
                      pl.BlockSpec((1, t, width), lambda b, hp, i: (b, 0, hp)),
                      pl.BlockSpec((1, n_blk, width, MOBA_BLOCK), lambda b, hp, i: (b, 0, hp, 0)),
                      pl.BlockSpec((1, hps, n_blk, MOBA_BLOCK), lambda b, hp, i: (b, hp, 0, i))],
            out_specs=pl.BlockSpec((1, MOBA_BLOCK, width), lambda b, hp, i: (b, i, hp)),
            scratch_shapes=[pltpu.VMEM((2, 2 * hps, MOBA_BLOCK, MOBA_BLOCK), F32)],
        ),
        out_shape=jax.ShapeDtypeStruct((bn, t, d), BF16),
        compiler_params=_cparams(("arbitrary",) * 3), name="moba_prompt",
    )(_alibi_slopes() * LOG2E, qt16, k16, vt16, pen)


def _matmul_kernel(a_ref, w_ref, o_ref):
    o_ref[...] = _wdot(a_ref[...], w_ref[0])


def _matmul(a, w, layer):
    rows, cols = a.shape[0], w.shape[2]
    return pl.pallas_call(
        _matmul_kernel, grid=(1,),
        in_specs=[pl.BlockSpec(a.shape, lambda g: (0, 0)),
                  pl.BlockSpec((1,) + w.shape[1:], lambda g: (layer, 0, 0))],
        out_specs=pl.BlockSpec((rows, cols), lambda g: (0, 0)),
        out_shape=jax.ShapeDtypeStruct((rows, cols), F32),
        compiler_params=_cparams(("arbitrary",)), name="matmul",
    )(a, w)


def _moba_sample_kernel(pt_ref, q_ref, kn_ref, vn_ref, slope_ref, *refs, past_len, blocks_per_step):
    del pt_ref
    n_pages = 2 * blocks_per_step
    k_pages, v_pages = refs[:n_pages], refs[n_pages:2 * n_pages]
    o_ref, q_lanes, m_s, l_s, g_s, o_s = refs[2 * n_pages:]
    step = pl.program_id(1)
    n_blk = pl.num_programs(1) * blocks_per_step
    page = k_pages[0].shape[-1]
    q_col = q_ref[0]
    blk_lane = lax.broadcasted_iota(jnp.int32, m_s.shape, 2)
    blk_lane_o = lax.broadcasted_iota(jnp.int32, o_s.shape, 2)

    @pl.when(step == 0)
    def _():
        q_lanes[...] = jnp.broadcast_to(q_col, q_lanes.shape)
        m_s[...] = jnp.zeros_like(m_s)
        l_s[...] = jnp.zeros_like(l_s)
        g_s[...] = jnp.zeros_like(g_s)
        o_s[...] = jnp.zeros_like(o_s)

    q_b = q_lanes[...]
    slope = slope_ref[...]
    tok = lax.broadcasted_iota(jnp.int32, (1, 1, page), 2)
    for j in range(blocks_per_step):
        n = step * blocks_per_step + j
        raw = [jnp.sum(k_pages[2 * j + half][0, 0] * q_b, axis=1, keepdims=True) for half in range(2)]
        s_pages = []
        for half, r in enumerate(raw):
            dist = (past_len - (n * MOBA_BLOCK + half * page) - tok).astype(F32)
            s_pages.append(r * (HEAD_DIM ** -0.5) - slope * dist)
        m = jnp.maximum(jnp.max(s_pages[0], axis=-1, keepdims=True),
                        jnp.max(s_pages[1], axis=-1, keepdims=True))
        l = jnp.zeros(m.shape, F32)
        pv = jnp.zeros(q_b.shape, F32)
        for half, s in enumerate(s_pages):
            p = jnp.exp(s - m)
            l = l + jnp.sum(p, axis=-1, keepdims=True)
            pv = pv + p * v_pages[2 * j + half][0, 0]
        gate = (jnp.sum(raw[0], axis=-1, keepdims=True)
                + jnp.sum(raw[1], axis=-1, keepdims=True)) * (1.0 / MOBA_BLOCK)
        m_s[...] = jnp.where(blk_lane == n, m, m_s[...])
        l_s[...] = jnp.where(blk_lane == n, l, l_s[...])
        g_s[...] = jnp.where(blk_lane == n, gate, g_s[...])
        o_s[...] = jnp.where(blk_lane_o == n, jnp.sum(pv, axis=-1, keepdims=True), o_s[...])

    @pl.when(step == pl.num_programs(1) - 1)
    def _():
        gates = jnp.where(blk_lane < n_blk, g_s[...], NEG_INF)
        sel = jnp.zeros(gates.shape, F32)
        for _ in range(MOBA_TOPK):
            g_max = jnp.max(gates, axis=-1, keepdims=True)
            idx = jnp.min(jnp.where(gates == g_max, blk_lane, LANES), axis=-1, keepdims=True)
            hit = blk_lane == idx
            sel = jnp.where(hit & (g_max > NEG_INF), 1.0, sel)
            gates = jnp.where(hit, NEG_INF, gates)
        chosen = sel > 0.0
        s_own = jnp.sum(q_col * kn_ref[0], axis=1, keepdims=True) * (HEAD_DIM ** -0.5)
        m_all = m_s[...]
        m_tot = jnp.maximum(jnp.max(jnp.where(chosen, m_all, NEG_INF), axis=-1, keepdims=True), s_own)
        w = jnp.where(chosen, jnp.exp(m_all - m_tot), 0.0)
        w_own = jnp.exp(s_own - m_tot)
        l_tot = jnp.sum(w * l_s[...], axis=-1, keepdims=True) + w_own
        o_tot = jnp.sum(w * o_s[...], axis=-1, keepdims=True) + w_own * vn_ref[0]
        o_ref[0] = o_tot / l_tot


def _moba_sample(q, k_new, v_new, cache_k, cache_v, page_table, layer):
    dec, n_heads, hd, _ = q.shape
    n_pages = page_table.shape[1]
    page = cache_k.shape[-1]
    assert MOBA_BLOCK == 2 * page
    n_blk = n_pages // 2
    bps = SAMPLE_BLOCKS_PER_STEP if n_blk % SAMPLE_BLOCKS_PER_STEP == 0 else 1
    past_len = n_pages * page
    vec = pl.BlockSpec((1, n_heads, hd, 1), lambda b, n, pt: (b, 0, 0, 0))

    def page_spec(idx):
        return pl.BlockSpec((1, 1, n_heads, hd, page),
                            lambda b, n, pt: (layer, pt[b, 2 * bps * n + idx], 0, 0, 0))

    page_specs = [page_spec(idx) for idx in range(2 * bps)]
    assert n_blk <= LANES
    stat = pltpu.VMEM((n_heads, 1, LANES), F32)
    return pl.pallas_call(
        functools.partial(_moba_sample_kernel, past_len=past_len, blocks_per_step=bps),
        grid_spec=pltpu.PrefetchScalarGridSpec(
            num_scalar_prefetch=1,
            grid=(dec, n_blk // bps),
            in_specs=[vec, vec, vec, pl.BlockSpec((n_heads, 1, 1), lambda b, n, pt: (0, 0, 0))]
            + page_specs + page_specs,
            out_specs=vec,
            scratch_shapes=[pltpu.VMEM((n_heads, hd, page), F32), stat, stat, stat,
                            pltpu.VMEM((n_heads, hd, LANES), F32)],
        ),
        out_shape=jax.ShapeDtypeStruct((dec, n_heads, hd, 1), F32),
        compiler_params=_cparams(("arbitrary", "arbitrary")), name="moba_sample",
    )(page_table, q, k_new, v_new, _alibi_slopes().reshape(n_heads, 1, 1),
      *([cache_k] * (2 * bps)), *([cache_v] * (2 * bps)))


def _trunk(x, mods, p, past):
    bn, t, d = x.shape
    ssm_re, ssm_im, new_k, new_v = [], [], [], []
    depth = len(mods)
    for layer in range(depth):
        i = layer // 2
        sh1, sc1, g1, sh2, sc2, g2 = mods[layer]
        act_dtype = p['attn_w_o'].dtype
        if layer % 2 == 0:
            u = _modulate(x, p['norm_mix'][layer], sh1, sc1, F32)
            if past is None:
                g, f_re, f_im = _s5_prompt(u, p['s5_bb16'][i], p['s5_cre16'][i], p['s5_cim16'][i],
                                           p['s5_pow_re'][i], p['s5_pow_im'][i], p['ssm_d'][i])
                ssm_re.append(f_re.reshape(bn, -1, SSM_STATE))
                ssm_im.append(f_im.reshape(bn, -1, SSM_STATE))
            else:
                h_re = past[3][i].reshape(t, -1)
                h_im = past[4][i].reshape(t, -1)
                g, s_re, s_im = _s5_sample(u[0], h_re, h_im, p['s5_bbar_re'][i], p['s5_bbar_im'][i],
                                           p['s5_c_re'][i], p['s5_c_im'][i],
                                           p['s5_pow_re'][i][0:1], p['s5_pow_im'][i][0:1], p['ssm_d'][i])
                g = g[None]
                ssm_re.append(s_re.reshape(t, -1, SSM_STATE))
                ssm_im.append(s_im.reshape(t, -1, SSM_STATE))
            x = _proj_residual(x, g, (p['ssm_w_glu_out'], p['ssm_w_glu_gate']), g1, i)
            h = _modulate(x, p['norm_ffn'][layer], sh2, sc2, act_dtype)
            x = _swiglu_residual(x, h, p['ffn_w_gate'][:, None], p['ffn_w_up'][:, None],
                                 p['ffn_w_down'][:, None], g2, i)
        else:
            h = _modulate(x, p['norm_mix'][layer], sh1, sc1, act_dtype)
            if past is None:
                k_t, v_t, k16, qt16, vt16, pen = _qkv_prompt(h, p['attn_w_qkv'], i)
                a = _moba_prompt(qt16, k16, vt16, pen)
                new_k.append(k_t.reshape(bn, N_HEADS, HEAD_DIM, t))
                new_v.append(v_t.reshape(bn, N_HEADS, HEAD_DIM, t))
            else:
                qkv = _matmul(h[0], p['attn_w_qkv'], i)
                q, k, v = (qkv[:, j * d:(j + 1) * d].reshape(t, N_HEADS, HEAD_DIM, 1) for j in range(3))
                a = _moba_sample(q, k, v, past[0], past[1], past[2], i).reshape(1, t, d)
                new_k.append(k.reshape(t, N_HEADS, HEAD_DIM, 1))
                new_v.append(v.reshape(t, N_HEADS, HEAD_DIM, 1))
            x = _proj_residual(x, a, (p['attn_w_o'],), g1, i)
            h, comb = _modulate(x, p['norm_ffn'][layer], sh2, sc2, act_dtype,
                                router=(p['moe_w_router'][i], p['moe_b_router'][i]))
            x = _swiglu_residual(x, h, p['moe_w_gate'], p['moe_w_up'], p['moe_w_down'], g2, i, combine=comb)
    y = _final_norm(x, p['norm_final'])
    new_k, new_v = (jnp.stack(z).transpose(0, 1, 4, 2, 3) for z in (new_k, new_v))
    return y, jnp.stack(ssm_re), jnp.stack(ssm_im), new_k, new_v


def kernel(x_prompt, x_sample, cache_k, cache_v, page_table, state_ssm_re, state_ssm_im,
           c_prompt, c_sample, ada_w, ada_b, norm_mix, norm_ffn, norm_final,
           ssm_a_re, ssm_a_im, ssm_log_dt, ssm_b_re, ssm_b_im, ssm_c_re, ssm_c_im, ssm_d,
           ssm_w_glu_out, ssm_w_glu_gate, attn_w_qkv, attn_w_o,
           ffn_w_gate, ffn_w_up, ffn_w_down,
           moe_w_router, moe_b_router, moe_w_gate, moe_w_up, moe_w_down):
    bn, t, d = x_prompt.shape
    dec = x_sample.shape[0]
    depth = ada_w.shape[0]
    assert x_sample.shape[1] == 1 and d == N_HEADS * HEAD_DIM and t % MOBA_BLOCK == 0

    n_cond = bn + dec
    c_all = jnp.pad(jnp.concatenate([c_prompt, c_sample], axis=0), ((0, -n_cond % SUBLANES), (0, 0)))
    mod = _adaln(c_all, ada_w, ada_b)
    mods_p = [[mod[l, :bn, j * d:(j + 1) * d].reshape(bn, 1, d) for j in range(6)] for l in range(depth)]
    mods_s = [[mod[l, bn:n_cond, j * d:(j + 1) * d].reshape(1, dec, d) for j in range(6)] for l in range(depth)]

    pow_re, pow_im, bbar_re, bbar_im = _s5_prep(ssm_a_re, ssm_a_im, ssm_log_dt, ssm_b_re, ssm_b_im,
                                                S5_TILE // SUBLANES)
    c_re_bd, c_im_bd = _blockdiag_c(ssm_c_re), _blockdiag_c(ssm_c_im)
    shared = {
        'norm_mix': norm_mix, 'norm_ffn': norm_ffn, 'norm_final': norm_final, 'ssm_d': ssm_d,
        's5_pow_re': pow_re, 's5_pow_im': pow_im, 's5_bbar_re': bbar_re, 's5_bbar_im': bbar_im,
        's5_bb16': jnp.concatenate([bbar_re, bbar_im], axis=-1).astype(BF16),
        's5_c_re': c_re_bd, 's5_c_im': c_im_bd,
        's5_cre16': c_re_bd.astype(BF16), 's5_cim16': c_im_bd.astype(BF16),
        'moe_w_router': moe_w_router, 'moe_b_router': moe_b_router,
    }
    big = {
        'ssm_w_glu_out': ssm_w_glu_out, 'ssm_w_glu_gate': ssm_w_glu_gate,
        'attn_w_qkv': attn_w_qkv, 'attn_w_o': attn_w_o,
        'ffn_w_gate': ffn_w_gate, 'ffn_w_up': ffn_w_up, 'ffn_w_down': ffn_w_down,
        'moe_w_gate': moe_w_gate, 'moe_w_up': moe_w_up, 'moe_w_down': moe_w_down,
    }
    p_prompt = dict(shared, **{name: w.astype(BF16) for name, w in big.items()})
    p_sample = dict(shared, **big)

    y_p, re_p, im_p, k_p, v_p = _trunk(x_prompt, mods_p, p_prompt, None)

    past = (cache_k.transpose(0, 1, 3, 4, 2), cache_v.transpose(0, 1, 3, 4, 2),
            page_table, state_ssm_re, state_ssm_im)
    y_s, re_s, im_s, k_s, v_s = _trunk(x_sample.reshape(1, dec, d), mods_s, p_sample, past)
    return (y_p, y_s.reshape(dec, 1, d), re_p, im_p, re_s, im_s, k_p, v_p, k_s, v_s)
```

```python
import functools

import jax
import jax.numpy as jnp
from jax import lax
from jax.experimental import pallas as pl
from jax.experimental.pallas import tpu as pltpu

F32 = jnp.float32
BF16 = jnp.bfloat16
HI = lax.Precision.HIGHEST
NEG_INF = float("-inf")
LOG2E = 1.4426950408889634

SUBLANES = 8
LANES = 128

RMS_EPS = 1e-6
SSM_GROUP = 16
SSM_STATE = 64
PACK_GROUPS = LANES // SSM_GROUP
PACK_STATES = PACK_GROUPS * SSM_STATE
N_HEADS = 16
HEAD_DIM = 64
MOBA_BLOCK = 256
MOBA_TOPK = 3
TOP_K = 2

ROW_TILE = 512
S5_TILE = 256
MOBA_HEADS_PER_STEP = 8
SAMPLE_BLOCKS_PER_STEP = 4
VMEM_LIMIT = 56 * 1024 * 1024


def _cparams(sem):
    return pltpu.CompilerParams(dimension_semantics=sem, vmem_limit_bytes=VMEM_LIMIT)


def _wdot(a, w):
    if w.dtype == F32:
        return jnp.dot(a.astype(F32), w, precision=HI, preferred_element_type=F32)
    return jnp.dot(a.astype(BF16), w, preferred_element_type=F32)


def _row_tile(t):
    return ROW_TILE if t % ROW_TILE == 0 else t


def _mod_spec(arr, tm):
    d = arr.shape[-1]
    if arr.shape[1] == 1:
        return pl.BlockSpec((1, 1, d), lambda b, t: (b, 0, 0))
    return pl.BlockSpec((1, tm, d), lambda b, t: (b, t, 0))


def _adaln_kernel(c_ref, w_ref, b_ref, o_ref):
    c = c_ref[...]
    s = c * jax.nn.sigmoid(c)
    o_ref[0] = jnp.dot(s, w_ref[0], precision=HI, preferred_element_type=F32) + b_ref[0]


def _adaln(c_all, ada_w, ada_b):
    depth, d, n = ada_w.shape
    r = c_all.shape[0]
    tn = 1536
    return pl.pallas_call(
        _adaln_kernel,
        grid=(depth, n // tn),
        in_specs=[pl.BlockSpec((r, d), lambda l, j: (0, 0)),
                  pl.BlockSpec((1, d, tn), lambda l, j: (l, 0, j)),
                  pl.BlockSpec((1, 1, tn), lambda l, j: (l, 0, j))],
        out_specs=pl.BlockSpec((1, r, tn), lambda l, j: (l, 0, j)),
        out_shape=jax.ShapeDtypeStruct((depth, r, n), F32),
        compiler_params=_cparams(("arbitrary", "arbitrary")),
        name="adaln",
    )(c_all, ada_w, ada_b.reshape(depth, 1, n))


def _rms(x, g):
    return x * lax.rsqrt(jnp.mean(x * x, axis=-1, keepdims=True) + RMS_EPS) * g


def _modulate_kernel(x_ref, g_ref, sh_ref, sc_ref, o_ref):
    h = _rms(x_ref[0], g_ref[...]) * (1.0 + sc_ref[0]) + sh_ref[0]
    o_ref[0] = h.astype(o_ref.dtype)


def _modulate_router_kernel(x_ref, g_ref, sh_ref, sc_ref, wr_ref, br_ref, o_ref, comb_ref):
    h = _rms(x_ref[0], g_ref[...]) * (1.0 + sc_ref[0]) + sh_ref[0]
    o_ref[0] = h.astype(o_ref.dtype)
    logits = jnp.dot(h, wr_ref[...], precision=HI, preferred_element_type=F32) + br_ref[...]
    lane = lax.broadcasted_iota(jnp.int32, logits.shape, 1)
    m1 = jnp.max(logits, axis=-1, keepdims=True)
    i1 = jnp.min(jnp.where(logits == m1, lane, LANES), axis=-1, keepdims=True)
    rest = jnp.where(lane == i1, NEG_INF, logits)
    m2 = jnp.max(rest, axis=-1, keepdims=True)
    i2 = jnp.min(jnp.where(rest == m2, lane, LANES), axis=-1, keepdims=True)
    e2 = jnp.exp(m2 - m1)
    den = 1.0 + e2
    comb_ref[0] = jnp.where(lane == i1, 1.0 / den, 0.0) + jnp.where(lane == i2, e2 / den, 0.0)


def _modulate(x, g, shift, scale, out_dtype, router=None):
    bn, t, d = x.shape
    tm = _row_tile(t)
    grid = (bn, t // tm)
    x_spec = pl.BlockSpec((1, tm, d), lambda b, i: (b, i, 0))
    in_specs = [x_spec, pl.BlockSpec((1, d), lambda b, i: (0, 0)), _mod_spec(shift, tm), _mod_spec(scale, tm)]
    args = [x, g.reshape(1, d), shift, scale]
    if router is None:
        return pl.pallas_call(
            _modulate_kernel, grid=grid, in_specs=in_specs, out_specs=x_spec,
            out_shape=jax.ShapeDtypeStruct((bn, t, d), out_dtype),
            compiler_params=_cparams(("arbitrary", "arbitrary")), name="modulate",
        )(*args)
    w_router, b_router = router
    n_exp = w_router.shape[1]
    wr = jnp.pad(w_router, ((0, 0), (0, LANES - n_exp)))
    br = jnp.pad(b_router.reshape(1, n_exp), ((0, 0), (0, LANES - n_exp)), constant_values=NEG_INF)
    in_specs += [pl.BlockSpec((d, LANES), lambda b, i: (0, 0)), pl.BlockSpec((1, LANES), lambda b, i: (0, 0))]
    return pl.pallas_call(
        _modulate_router_kernel, grid=grid, in_specs=in_specs,
        out_specs=[x_spec, pl.BlockSpec((1, tm, LANES), lambda b, i: (b, i, 0))],
        out_shape=[jax.ShapeDtypeStruct((bn, t, d), out_dtype), jax.ShapeDtypeStruct((bn, t, LANES), F32)],
        compiler_params=_cparams(("arbitrary", "arbitrary")), name="modulate_router",
    )(*args, wr, br)


def _store_residual(x_new, out_refs):
    if len(out_refs) == 1:
        out_refs[0][0] = x_new
        return
    g_ref, sh_ref, sc_ref, o_ref, h_ref = out_refs
    o_ref[0] = x_new
    h_ref[0] = (_rms(x_new, g_ref[...]) * (1.0 + sc_ref[0]) + sh_ref[0]).astype(h_ref.dtype)


def _proj_residual_kernel(x_ref, a_ref, w_ref, gate_ref, *out_refs):
    y = _wdot(a_ref[0], w_ref[0])
    _store_residual(x_ref[0] + gate_ref[0] * y, out_refs)


def _glu_residual_kernel(x_ref, a_ref, wo_ref, wg_ref, gate_ref, *out_refs):
    a = a_ref[0]
    y = _wdot(a, wo_ref[0])
    z = _wdot(a, wg_ref[0])
    _store_residual(x_ref[0] + gate_ref[0] * (y * jax.nn.sigmoid(z)), out_refs)


def _next_modulate_io(nxt, d, tm, x_spec, shape, const_map, mod_spec):
    if nxt is None:
        return [], [], x_spec, jax.ShapeDtypeStruct(shape, F32)
    g, shift, scale, dtype = nxt
    args = [g.reshape(1, d), shift, scale]
    specs = [pl.BlockSpec((1, d), const_map), mod_spec(shift), mod_spec(scale)]
    return args, specs, [x_spec, x_spec], [jax.ShapeDtypeStruct(shape, F32), jax.ShapeDtypeStruct(shape, dtype)]


def _proj_residual(x, a, weights, gate, layer, nxt=None):
    bn, t, d = x.shape
    tm = _row_tile(t)
    x_spec = pl.BlockSpec((1, tm, d), lambda b, i: (b, i, 0))
    w_spec = pl.BlockSpec((1, d, d), lambda b, i: (layer, 0, 0))
    kern = _proj_residual_kernel if len(weights) == 1 else _glu_residual_kernel
    n_args, n_specs, out_specs, out_shape = _next_modulate_io(
        nxt, d, tm, x_spec, (bn, t, d), lambda b, i: (0, 0), lambda m: _mod_spec(m, tm))
    return pl.pallas_call(
        kern, grid=(bn, t // tm),
        in_specs=[x_spec, x_spec] + [w_spec] * len(weights) + [_mod_spec(gate, tm)] + n_specs,
        out_specs=out_specs, out_shape=out_shape,
        compiler_params=_cparams(("arbitrary", "arbitrary")), name="proj_residual",
    )(x, a, *weights, gate, *n_args)


def _swiglu_kernel(x_ref, h_ref, wg_ref, wu_ref, wd_ref, gate_ref, *rest):
    out_refs, acc_ref = rest[:-1], rest[-1]
    e, f = pl.program_id(2), pl.program_id(3)

    @pl.when((e == 0) & (f == 0))
    def _():
        acc_ref[...] = jnp.zeros_like(acc_ref)

    h = h_ref[0]
    a = _wdot(h, wg_ref[0, 0])
    u = _wdot(h, wu_ref[0, 0])
    acc_ref[...] += _wdot(a * jax.nn.sigmoid(a) * u, wd_ref[0, 0])

    @pl.when((e == pl.num_programs(2) - 1) & (f == pl.num_programs(3) - 1))
    def _():
        _store_residual(x_ref[0] + gate_ref[0] * acc_ref[...], out_refs)


def _moe_swiglu_kernel(x_ref, h_ref, comb_ref, wg_ref, wu_ref, wd_ref, gate_ref, *rest):
    out_refs, acc_ref = rest[:-1], rest[-1]
    e, f = pl.program_id(2), pl.program_id(3)

    @pl.when((e == 0) & (f == 0))
    def _():
        acc_ref[...] = jnp.zeros_like(acc_ref)

    h = h_ref[0]
    a = _wdot(h, wg_ref[0, 0])
    u = _wdot(h, wu_ref[0, 0])
    act = a * jax.nn.sigmoid(a) * u
    comb = comb_ref[0]
    lane = lax.broadcasted_iota(jnp.int32, comb.shape, 1)
    w_e = jnp.sum(jnp.where(lane == e, comb, 0.0), axis=-1, keepdims=True)
    acc_ref[...] += w_e * _wdot(act, wd_ref[0, 0])

    @pl.when((e == pl.num_programs(2) - 1) & (f == pl.num_programs(3) - 1))
    def _():
        _store_residual(x_ref[0] + gate_ref[0] * acc_ref[...], out_refs)


def _swiglu_residual(x, h, w_gate, w_up, w_down, gate, layer, combine=None, nxt=None):
    bn, t, d = x.shape
    _, n_exp, _, ff = w_gate.shape
    tm = _row_tile(t)
    tf = 1408
    assert ff % tf == 0
    x_spec = pl.BlockSpec((1, tm, d), lambda b, i, e, f: (b, i, 0))
    wi_spec = pl.BlockSpec((1, 1, d, tf), lambda b, i, e, f: (layer, e, 0, f))
    wd_spec = pl.BlockSpec((1, 1, tf, d), lambda b, i, e, f: (layer, e, f, 0))
    def mod_spec(arr):
        if arr.shape[1] == 1:
            return pl.BlockSpec((1, 1, d), lambda b, i, e, f: (b, 0, 0))
        return pl.BlockSpec((1, tm, d), lambda b, i, e, f: (b, i, 0))

    n_args, n_specs, out_specs, out_shape = _next_modulate_io(
        nxt, d, tm, x_spec, (bn, t, d), lambda b, i, e, f: (0, 0), mod_spec)
    if combine is None:
        kern, in_specs, args = _swiglu_kernel, [x_spec, x_spec], [x, h]
    else:
        kern = _moe_swiglu_kernel
        in_specs = [x_spec, x_spec, pl.BlockSpec((1, tm, LANES), lambda b, i, e, f: (b, i, 0))]
        args = [x, h, combine]
    return pl.pallas_call(
        kern, grid=(bn, t // tm, n_exp, ff // tf),
        in_specs=in_specs + [wi_spec, wi_spec, wd_spec, mod_spec(gate)] + n_specs,
        out_specs=out_specs, out_shape=out_shape,
        scratch_shapes=[pltpu.VMEM((tm, d), F32)],
        compiler_params=_cparams(("arbitrary",) * 4), name="swiglu",
    )(*args, w_gate, w_up, w_down, gate, *n_args)


def _s5_prep_kernel(are_ref, aim_ref, ldt_ref, bre_ref, bim_ref,
                    pre_ref, pim_ref, obr_ref, obi_ref, *, seg):
    a_re, a_im = are_ref[0], aim_ref[0]
    dt = jnp.exp(ldt_ref[0])
    steps = lax.broadcasted_iota(jnp.int32, (seg, PACK_STATES), 0).astype(F32) + 1.0
    mag = jnp.exp(steps * (dt * a_re))
    ang = steps * (dt * a_im)
    pre_ref[0] = mag * jnp.cos(ang)
    pim_ref[0] = mag * jnp.sin(ang)
    mag1 = jnp.exp(dt * a_re)
    ab_re = mag1 * jnp.cos(dt * a_im)
    ab_im = mag1 * jnp.sin(dt * a_im)
    den = a_re * a_re + a_im * a_im
    num_re = ab_re - 1.0
    coef_re = (num_re * a_re + ab_im * a_im) / den
    coef_im = (ab_im * a_re - num_re * a_im) / den
    b_re, b_im = bre_ref[0, 0], bim_ref[0, 0]
    obr_ref[0, 0] = coef_re * b_re - coef_im * b_im
    obi_ref[0, 0] = coef_re * b_im + coef_im * b_re


def _blockdiag_b(b):
    nl, g, p, j = b.shape
    b5 = b.reshape(nl, g // PACK_GROUPS, PACK_GROUPS, p, j).transpose(0, 1, 2, 4, 3)
    eye = jnp.eye(PACK_GROUPS, dtype=b.dtype)
    out = b5[:, :, :, :, None, :] * eye[None, None, :, None, :, None]
    return out.reshape(nl, g // PACK_GROUPS, PACK_GROUPS * j, PACK_GROUPS * p)


def _blockdiag_c(c):
    nl, g, j, p = c.shape
    c5 = c.reshape(nl, g // PACK_GROUPS, PACK_GROUPS, j, p).transpose(0, 1, 2, 4, 3)
    eye = jnp.eye(PACK_GROUPS, dtype=c.dtype)
    out = c5[:, :, :, :, None, :] * eye[None, None, :, None, :, None]
    return out.reshape(nl, g // PACK_GROUPS, PACK_GROUPS * p, PACK_GROUPS * j)


def _s5_prep(a_re, a_im, log_dt, b_re, b_im, seg):
    nl, g, p = a_re.shape
    n_state = g * p
    n_pack = g // PACK_GROUPS
    row = lambda a: a.reshape(nl, 1, n_state)
    ldt = jnp.broadcast_to(log_dt[:, :, None], (nl, g, p))
    braw_re, braw_im = _blockdiag_b(b_re), _blockdiag_b(b_im)
    vec_spec = pl.BlockSpec((1, 1, PACK_STATES), lambda l, k: (l, 0, k))
    mat_spec = pl.BlockSpec((1, 1, LANES, PACK_STATES), lambda l, k: (l, k, 0, 0))
    pow_spec = pl.BlockSpec((1, seg, PACK_STATES), lambda l, k: (l, 0, k))
    return pl.pallas_call(
        functools.partial(_s5_prep_kernel, seg=seg),
        grid=(nl, n_pack),
        in_specs=[vec_spec, vec_spec, vec_spec, mat_spec, mat_spec],
        out_specs=[pow_spec, pow_spec, mat_spec, mat_spec],
        out_shape=[jax.ShapeDtypeStruct((nl, seg, n_state), F32)] * 2
        + [jax.ShapeDtypeStruct(braw_re.shape, F32)] * 2,
        compiler_params=_cparams(("arbitrary", "arbitrary")), name="s5_prep",
    )(row(a_re), row(a_im), row(ldt), braw_re, braw_im)


def _s5_prompt_kernel(u_ref, bb_ref, cre_ref, cim_ref, pre_ref, pim_ref, d_ref,
                      g_ref, fre_ref, fim_ref,
                      uperm, sre, sim, gperm, car_re, car_im, ini_re, ini_im, *, seg, chunk):
    n_pack = bb_ref.shape[0]
    n_state = sre.shape[1]

    @pl.when(pl.program_id(1) == 0)
    def _():
        car_re[...] = jnp.zeros_like(car_re)
        car_im[...] = jnp.zeros_like(car_im)

    for pk in range(n_pack):
        for k in range(SUBLANES):
            uperm[pk, pl.ds(k, seg, stride=SUBLANES), :] = (
                u_ref[0, k * seg:(k + 1) * seg, pk * LANES:(pk + 1) * LANES])

    for pk in range(n_pack):
        z = jnp.dot(uperm[pk].astype(BF16), bb_ref[pk], preferred_element_type=F32)
        sre[:, pk * PACK_STATES:(pk + 1) * PACK_STATES] = z[:, :PACK_STATES]
        sim[:, pk * PACK_STATES:(pk + 1) * PACK_STATES] = z[:, PACK_STATES:]

    for c in range(n_state // chunk):
        cs = slice(c * chunk, (c + 1) * chunk)
        lam_re = jnp.broadcast_to(pre_ref[0:1, cs], (SUBLANES, chunk))
        lam_im = jnp.broadcast_to(pim_ref[0:1, cs], (SUBLANES, chunk))

        def scan_step(n, carry, cs=cs, lam_re=lam_re, lam_im=lam_im):
            h_re, h_im = carry
            rows = pl.ds(pl.multiple_of(n * SUBLANES, SUBLANES), SUBLANES)
            n_re = lam_re * h_re - lam_im * h_im + sre[rows, cs]
            n_im = lam_re * h_im + lam_im * h_re + sim[rows, cs]
            sre[rows, cs] = n_re
            sim[rows, cs] = n_im
            return n_re, n_im

        zero = jnp.zeros((SUBLANES, chunk), F32)
        end_re, end_im = lax.fori_loop(0, seg, scan_step, (zero, zero), unroll=4)

        seg_re, seg_im = pre_ref[seg - 1:seg, cs], pim_ref[seg - 1:seg, cs]
        cur_re, cur_im = car_re[:, cs], car_im[:, cs]
        for k in range(SUBLANES):
            ini_re[k:k + 1, cs] = cur_re
            ini_im[k:k + 1, cs] = cur_im
            nxt_re = end_re[k:k + 1] + seg_re * cur_re - seg_im * cur_im
            nxt_im = end_im[k:k + 1] + seg_re * cur_im + seg_im * cur_re
            cur_re, cur_im = nxt_re, nxt_im
        car_re[:, cs] = cur_re
        car_im[:, cs] = cur_im

        in_re, in_im = ini_re[:, cs], ini_im[:, cs]

        def fix_step(n, carry, cs=cs, in_re=in_re, in_im=in_im):
            rows = pl.ds(pl.multiple_of(n * SUBLANES, SUBLANES), SUBLANES)
            p_re, p_im = pre_ref[pl.ds(n, 1), cs], pim_ref[pl.ds(n, 1), cs]
            sre[rows, cs] = sre[rows, cs] + p_re * in_re - p_im * in_im
            sim[rows, cs] = sim[rows, cs] + p_re * in_im + p_im * in_re
            return carry

        lax.fori_loop(0, seg, fix_step, 0, unroll=4)

    fre_ref[0] = car_re[...]
    fim_ref[0] = car_im[...]

    for pk in range(n_pack):
        ss = slice(pk * PACK_STATES, (pk + 1) * PACK_STATES)
        ls = slice(pk * LANES, (pk + 1) * LANES)
        y = (jnp.dot(sre[:, ss].astype(BF16), cre_ref[pk], preferred_element_type=F32)
             - jnp.dot(sim[:, ss].astype(BF16), cim_ref[pk], preferred_element_type=F32))
        y = y + d_ref[:, ls] * uperm[pk]
        gperm[pk] = jax.nn.gelu(y, approximate=True)

    for pk in range(n_pack):
        for k in range(SUBLANES):
            g_ref[0, k * seg:(k + 1) * seg, pk * LANES:(pk + 1) * LANES] = (
                gperm[pk, pl.ds(k, seg, stride=SUBLANES), :].astype(g_ref.dtype))


def _s5_prompt(u, bb16, cre16, cim16, pow_re, pow_im, d_skip):
    bn, t, d = u.shape
    n_pack = bb16.shape[0]
    n_state = pow_re.shape[1]
    tt = S5_TILE
    seg = tt // SUBLANES
    assert t % tt == 0 and pow_re.shape[0] == seg
    const = lambda shape: pl.BlockSpec(shape, lambda b, i: (0,) * len(shape))
    u_spec = pl.BlockSpec((1, tt, d), lambda b, i: (b, i, 0))
    fin_spec = pl.BlockSpec((1, 1, n_state), lambda b, i: (b, 0, 0))
    return pl.pallas_call(
        functools.partial(_s5_prompt_kernel, seg=seg, chunk=512),
        grid=(bn, t // tt),
        in_specs=[u_spec, const(bb16.shape), const(cre16.shape), const(cim16.shape),
                  const(pow_re.shape), const(pow_im.shape), const((1, d))],
        out_specs=[u_spec, fin_spec, fin_spec],
        out_shape=[jax.ShapeDtypeStruct((bn, t, d), BF16),
                   jax.ShapeDtypeStruct((bn, 1, n_state), F32),
                   jax.ShapeDtypeStruct((bn, 1, n_state), F32)],
        scratch_shapes=[pltpu.VMEM((n_pack, tt, LANES), F32), pltpu.VMEM((tt, n_state), F32),
                        pltpu.VMEM((tt, n_state), F32), pltpu.VMEM((n_pack, tt, LANES), F32),
                        pltpu.VMEM((1, n_state), F32), pltpu.VMEM((1, n_state), F32),
                        pltpu.VMEM((SUBLANES, n_state), F32), pltpu.VMEM((SUBLANES, n_state), F32)],
        compiler_params=_cparams(("arbitrary", "arbitrary")), name="s5_prompt",
    )(u, bb16, cre16, cim16, pow_re, pow_im, d_skip.reshape(1, d))


def _s5_sample_kernel(u_ref, hre_ref, him_ref, bre_ref, bim_ref, cre_ref, cim_ref,
                      lre_ref, lim_ref, d_ref, g_ref, sre_ref, sim_ref):
    for pk in range(bre_ref.shape[0]):
        ss = slice(pk * PACK_STATES, (pk + 1) * PACK_STATES)
        ls = slice(pk * LANES, (pk + 1) * LANES)
        u = u_ref[:, ls]
        lam_re, lam_im = lre_ref[:, ss], lim_ref[:, ss]
        h_re, h_im = hre_ref[:, ss], him_ref[:, ss]
        s_re = (jnp.dot(u, bre_ref[pk], precision=HI, preferred_element_type=F32)
                + (lam_re * h_re - lam_im * h_im))
        s_im = (jnp.dot(u, bim_ref[pk], precision=HI, preferred_element_type=F32)
                + (lam_re * h_im + lam_im * h_re))
        sre_ref[:, ss] = s_re
        sim_ref[:, ss] = s_im
        y = (jnp.dot(s_re, cre_ref[pk], precision=HI, preferred_element_type=F32)
             - jnp.dot(s_im, cim_ref[pk], precision=HI, preferred_element_type=F32))
        y = y + d_ref[:, ls] * u
        g_ref[:, ls] = jax.nn.gelu(y, approximate=True).astype(g_ref.dtype)


def _s5_sample(u, h_re, h_im, bbar_re, bbar_im, c_re, c_im, lam_re, lam_im, d_skip):
    rows, d = u.shape
    n_state = h_re.shape[1]
    return pl.pallas_call(
        _s5_sample_kernel,
        out_shape=[jax.ShapeDtypeStruct((rows, d), F32),
                   jax.ShapeDtypeStruct((rows, n_state), F32),
                   jax.ShapeDtypeStruct((rows, n_state), F32)],
        compiler_params=pltpu.CompilerParams(vmem_limit_bytes=VMEM_LIMIT), name="s5_sample",
    )(u, h_re, h_im, bbar_re, bbar_im, c_re, c_im, lam_re, lam_im, d_skip.reshape(1, d))


def _top_blocks_penalty(gate, blk, n_valid_below):
    gate = jnp.where(blk < n_valid_below, gate, NEG_INF)
    n_blk = gate.shape[0]
    sel = jnp.zeros(gate.shape, F32)
    for _ in range(MOBA_TOPK):
        m = jnp.max(gate, axis=0, keepdims=True)
        idx = jnp.min(jnp.where(gate == m, blk, n_blk), axis=0, keepdims=True)
        hit = blk == idx
        sel = jnp.where(hit & (m > NEG_INF), 1.0, sel)
        gate = jnp.where(hit, NEG_INF, gate)
    return jnp.where(sel > 0.0, 0.0, NEG_INF)


def _qkv_prompt_kernel(h_ref, w_ref, kt_ref, vt_ref, k16_ref, qt16_ref, vt16_ref, pen_ref, kmean):
    i = pl.program_id(1)
    d = k16_ref.shape[-1]
    n_blk = kmean.shape[0]

    @pl.when(i == 0)
    def _():
        kmean[...] = jnp.zeros_like(kmean)

    r = jnp.dot(h_ref[0], w_ref[0], preferred_element_type=F32)
    q, k, v = r[:, :d], r[:, d:2 * d], r[:, 2 * d:]
    k16_ref[0] = k.astype(BF16)
    q_t, v_t = q.T, v.T
    kt_ref[0] = k.T
    vt_ref[0] = v_t
    qt16_ref[0] = (q_t * (LOG2E * HEAD_DIM ** -0.5)).astype(BF16)
    vt16_ref[0, 0] = v_t.astype(BF16)

    km = kmean[...]
    blk = lax.broadcasted_iota(jnp.int32, (n_blk, MOBA_BLOCK), 0)
    for h in range(N_HEADS):
        hs = slice(h * HEAD_DIM, (h + 1) * HEAD_DIM)
        gate = jnp.dot(km[:, hs], q_t[hs, :], precision=HI, preferred_element_type=F32)
        pen_ref[0, h] = _top_blocks_penalty(gate, blk, i)

    row = lax.broadcasted_iota(jnp.int32, km.shape, 0)
    kmean[...] = jnp.where(row == i, jnp.mean(k, axis=0, keepdims=True), km)


def _qkv_prompt(h16, w16, layer):
    bn, t, d = h16.shape
    n_blk = t // MOBA_BLOCK
    tile = pl.BlockSpec((1, MOBA_BLOCK, d), lambda b, i: (b, i, 0))
    tile_t = pl.BlockSpec((1, d, MOBA_BLOCK), lambda b, i: (b, 0, i))
    return pl.pallas_call(
        _qkv_prompt_kernel, grid=(bn, n_blk),
        in_specs=[tile, pl.BlockSpec((1,) + w16.shape[1:], lambda b, i: (layer, 0, 0))],
        out_specs=[tile_t, tile_t, tile, tile_t,
                   pl.BlockSpec((1, 1, d, MOBA_BLOCK), lambda b, i: (b, i, 0, 0)),
                   pl.BlockSpec((1, N_HEADS, n_blk, MOBA_BLOCK), lambda b, i: (b, 0, 0, i))],
        out_shape=[jax.ShapeDtypeStruct((bn, d, t), F32), jax.ShapeDtypeStruct((bn, d, t), F32),
                   jax.ShapeDtypeStruct((bn, t, d), BF16),
                   jax.ShapeDtypeStruct((bn, d, t), BF16),
                   jax.ShapeDtypeStruct((bn, n_blk, d, MOBA_BLOCK), BF16),
                   jax.ShapeDtypeStruct((bn, N_HEADS, n_blk, t), F32)],
        scratch_shapes=[pltpu.VMEM((n_blk, d), F32)],
        compiler_params=_cparams(("arbitrary", "arbitrary")), name="qkv_prompt",
    )(h16, w16)


def _moba_prompt_kernel(slope_ref, qt_ref, k_ref, vt_ref, pen_ref, o_ref, s_ring, *, n_heads):
    hp, i = pl.program_id(1), pl.program_id(2)
    blk = MOBA_BLOCK
    k_iota = lax.broadcasted_iota(jnp.int32, (blk, blk), 0)
    q_iota = lax.broadcasted_iota(jnp.int32, (blk, blk), 1)
    causal = k_iota <= q_iota
    k_local = k_iota.astype(F32)
    head_row = lax.broadcasted_iota(jnp.int32, (LANES, blk), 0) // HEAD_DIM
    heads = range(n_heads)
    slab = [slice((a // 2) * LANES, (a // 2 + 1) * LANES) for a in heads]
    q_h = []
    for a in heads:
        q_slab = qt_ref[0, slab[a], :]
        q_h.append(jnp.where(head_row == a % 2, q_slab, jnp.zeros_like(q_slab)))
    slope = [slope_ref[n_heads * hp + a] for a in heads]
    k_bias = [slope[a] * k_local for a in heads]
    rows = [slice(a * HEAD_DIM, (a + 1) * HEAD_DIM) for a in heads]

    def scores(a, n):
        k_n = k_ref[0, pl.ds(pl.multiple_of(n * blk, blk), blk), slab[a]]
        return jnp.dot(k_n, q_h[a], preferred_element_type=F32) + k_bias[a]

    def offset(a, n):
        shift = slope[a] * jnp.full((1, blk), (n - i) * blk, jnp.int32).astype(F32)
        return shift + pen_ref[0, a, pl.ds(n, 1), :]

    ones_rows = jnp.ones((2 * SUBLANES, blk), BF16)

    def weighted_values(a, n, p):
        v_aug = jnp.concatenate([vt_ref[0, n, rows[a], :], ones_rows], axis=0)
        return jnp.dot(v_aug, p.astype(BF16), preferred_element_type=F32)

    init = []
    for a in heads:
        s = jnp.where(causal, scores(a, i), NEG_INF)
        m0 = jnp.max(s, axis=0, keepdims=True)
        init += [m0, weighted_values(a, i, jnp.exp2(s - m0))]

    def update(carry, blocks, get_scores):
        out = []
        for a in heads:
            m, acc = carry[2 * a:2 * a + 2]
            off_blk = [offset(a, n) for n in blocks]
            m_new = m
            for k, off in enumerate(off_blk):
                m_new = jnp.maximum(m_new, jnp.max(get_scores(a, k), axis=0, keepdims=True) + off)
            acc = jnp.exp2(m - m_new) * acc
            for k, (n, off) in enumerate(zip(blocks, off_blk)):
                acc = acc + weighted_values(a, n, jnp.exp2(get_scores(a, k) - (m_new - off)))
            out += [m_new, acc]
        return tuple(out)

    n_pairs = i // 2

    def store_pair_scores(slot, j):
        for k, n in enumerate((jnp.minimum(2 * j, i), jnp.minimum(2 * j + 1, i))):
            for a in heads:
                s_ring[slot, 2 * a + k] = scores(a, n)

    def pair_step(j, state):
        slot = j & 1
        state = update(state, (2 * j, 2 * j + 1), lambda a, k: s_ring[slot, 2 * a + k])
        store_pair_scores(1 - slot, j + 1)
        return state

    store_pair_scores(0, 0)
    carry = lax.fori_loop(0, n_pairs, pair_step, tuple(init))

    def single_step(n, state):
        tiles = [scores(a, n) for a in heads]
        return update(state, (n,), lambda a, k: tiles[a])

    carry = lax.fori_loop(2 * n_pairs, i, single_step, carry)
    outs = [carry[2 * a + 1][:HEAD_DIM] / carry[2 * a + 1][HEAD_DIM:HEAD_DIM + 1] for a in heads]
    o_ref[0] = jnp.concatenate(outs, axis=0).T.astype(o_ref.dtype)


def _alibi_slopes():
    return 2.0 ** (-8.0 * jnp.arange(1, N_HEADS + 1, dtype=F32) / N_HEADS)


def _moba_prompt(qt16, k16, vt16, pen):
    bn, d, t = qt16.shape
    n_blk = t // MOBA_BLOCK
    hps = MOBA_HEADS_PER_STEP
    width = hps * HEAD_DIM
    return pl.pallas_call(
        functools.partial(_moba_prompt_kernel, n_heads=hps),
        grid_spec=pltpu.PrefetchScalarGridSpec(
            num_scalar_prefetch=0,
            grid=(bn, N_HEADS // hps, n_blk),
            in_specs=[pl.BlockSpec(memory_space=pltpu.SMEM),
                      pl.BlockSpec((1, width, MOBA_BLOCK), lambda b, hp, i: (b, hp, i)),
                      pl.BlockSpec((1, t, width), lambda b, hp, i: (b, 0, hp)),
                      pl.BlockSpec((1, n_blk, width, MOBA_BLOCK), lambda b, hp, i: (b, 0, hp, 0)),
                      pl.BlockSpec((1, hps, n_blk, MOBA_BLOCK), lambda b, hp, i: (b, hp, 0, i))],
            out_specs=pl.BlockSpec((1, MOBA_BLOCK, width), lambda b, hp, i: (b, i, hp)),
            scratch_shapes=[pltpu.VMEM((2, 2 * hps, MOBA_BLOCK, MOBA_BLOCK), F32)],
        ),
        out_shape=jax.ShapeDtypeStruct((bn, t, d), BF16),
        compiler_params=_cparams(("arbitrary",) * 3), name="moba_prompt",
    )(_alibi_slopes() * LOG2E, qt16, k16, vt16, pen)


def _matmul_kernel(a_ref, w_ref, o_ref):
    o_ref[...] = _wdot(a_ref[...], w_ref[0])


def _matmul(a, w, layer):
    rows, cols = a.shape[0], w.shape[2]
    return pl.pallas_call(
        _matmul_kernel, grid=(1,),
        in_specs=[pl.BlockSpec(a.shape, lambda g: (0, 0)),
                  pl.BlockSpec((1,) + w.shape[1:], lambda g: (layer, 0, 0))],
        out_specs=pl.BlockSpec((rows, cols), lambda g: (0, 0)),
        out_shape=jax.ShapeDtypeStruct((rows, cols), F32),
        compiler_params=_cparams(("arbitrary",)), name="matmul",
    )(a, w)


def _moba_sample_kernel(pt_ref, q_ref, kn_ref, vn_ref, slope_ref, *refs, past_len, blocks_per_step):
    del pt_ref
    n_pages = 2 * blocks_per_step
    k_pages, v_pages = refs[:n_pages], refs[n_pages:2 * n_pages]
    o_ref, q_lanes, m_s, l_s, g_s, o_s = refs[2 * n_pages:]
    step = pl.program_id(1)
    n_blk = pl.num_programs(1) * blocks_per_step
    page = k_pages[0].shape[-1]
    q_col = q_ref[0]
    blk_lane = lax.broadcasted_iota(jnp.int32, m_s.shape, 2)
    blk_lane_o = lax.broadcasted_iota(jnp.int32, o_s.shape, 2)

    @pl.when(step == 0)
    def _():
        q_lanes[...] = jnp.broadcast_to(q_col, q_lanes.shape)
        m_s[...] = jnp.zeros_like(m_s)
        l_s[...] = jnp.zeros_like(l_s)
        g_s[...] = jnp.zeros_like(g_s)
        o_s[...] = jnp.zeros_like(o_s)

    q_b = q_lanes[...]
    slope = slope_ref[...]
    tok = lax.broadcasted_iota(jnp.int32, (1, 1, page), 2)
    for j in range(blocks_per_step):
        n = step * blocks_per_step + j
        raw = [jnp.sum(k_pages[2 * j + half][0, 0] * q_b, axis=1, keepdims=True) for half in range(2)]
        s_pages = []
        for half, r in enumerate(raw):
            dist = (past_len - (n * MOBA_BLOCK + half * page) - tok).astype(F32)
            s_pages.append(r * (HEAD_DIM ** -0.5) - slope * dist)
        m = jnp.maximum(jnp.max(s_pages[0], axis=-1, keepdims=True),
                        jnp.max(s_pages[1], axis=-1, keepdims=True))
        l = jnp.zeros(m.shape, F32)
        pv = jnp.zeros(q_b.shape, F32)
        for half, s in enumerate(s_pages):
            p = jnp.exp(s - m)
            l = l + jnp.sum(p, axis=-1, keepdims=True)
            pv = pv + p * v_pages[2 * j + half][0, 0]
        gate = (jnp.sum(raw[0], axis=-1, keepdims=True)
                + jnp.sum(raw[1], axis=-1, keepdims=True)) * (1.0 / MOBA_BLOCK)
        m_s[...] = jnp.where(blk_lane == n, m, m_s[...])
        l_s[...] = jnp.where(blk_lane == n, l, l_s[...])
        g_s[...] = jnp.where(blk_lane == n, gate, g_s[...])
        o_s[...] = jnp.where(blk_lane_o == n, jnp.sum(pv, axis=-1, keepdims=True), o_s[...])

    @pl.when(step == pl.num_programs(1) - 1)
    def _():
        gates = jnp.where(blk_lane < n_blk, g_s[...], NEG_INF)
        sel = jnp.zeros(gates.shape, F32)
        for _ in range(MOBA_TOPK):
            g_max = jnp.max(gates, axis=-1, keepdims=True)
            idx = jnp.min(jnp.where(gates == g_max, blk_lane, LANES), axis=-1, keepdims=True)
            hit = blk_lane == idx
            sel = jnp.where(hit & (g_max > NEG_INF), 1.0, sel)
            gates = jnp.where(hit, NEG_INF, gates)
        chosen = sel > 0.0
        s_own = jnp.sum(q_col * kn_ref[0], axis=1, keepdims=True) * (HEAD_DIM ** -0.5)
        m_all = m_s[...]
        m_tot = jnp.maximum(jnp.max(jnp.where(chosen, m_all, NEG_INF), axis=-1, keepdims=True), s_own)
        w = jnp.where(chosen, jnp.exp(m_all - m_tot), 0.0)
        w_own = jnp.exp(s_own - m_tot)
        l_tot = jnp.sum(w * l_s[...], axis=-1, keepdims=True) + w_own
        o_tot = jnp.sum(w * o_s[...], axis=-1, keepdims=True) + w_own * vn_ref[0]
        o_ref[0] = o_tot / l_tot


def _moba_sample(q, k_new, v_new, cache_k, cache_v, page_table, layer):
    dec, n_heads, hd, _ = q.shape
    n_pages = page_table.shape[1]
    page = cache_k.shape[-1]
    assert MOBA_BLOCK == 2 * page
    n_blk = n_pages // 2
    bps = SAMPLE_BLOCKS_PER_STEP if n_blk % SAMPLE_BLOCKS_PER_STEP == 0 else 1
    past_len = n_pages * page
    vec = pl.BlockSpec((1, n_heads, hd, 1), lambda b, n, pt: (b, 0, 0, 0))

    def page_spec(idx):
        return pl.BlockSpec((1, 1, n_heads, hd, page),
                            lambda b, n, pt: (layer, pt[b, 2 * bps * n + idx], 0, 0, 0))

    page_specs = [page_spec(idx) for idx in range(2 * bps)]
    assert n_blk <= LANES
    stat = pltpu.VMEM((n_heads, 1, LANES), F32)
    return pl.pallas_call(
        functools.partial(_moba_sample_kernel, past_len=past_len, blocks_per_step=bps),
        grid_spec=pltpu.PrefetchScalarGridSpec(
            num_scalar_prefetch=1,
            grid=(dec, n_blk // bps),
            in_specs=[vec, vec, vec, pl.BlockSpec((n_heads, 1, 1), lambda b, n, pt: (0, 0, 0))]
            + page_specs + page_specs,
            out_specs=vec,
            scratch_shapes=[pltpu.VMEM((n_heads, hd, page), F32), stat, stat, stat,
                            pltpu.VMEM((n_heads, hd, LANES), F32)],
        ),
        out_shape=jax.ShapeDtypeStruct((dec, n_heads, hd, 1), F32),
        compiler_params=_cparams(("arbitrary", "arbitrary")), name="moba_sample",
    )(page_table, q, k_new, v_new, _alibi_slopes().reshape(n_heads, 1, 1),
      *([cache_k] * (2 * bps)), *([cache_v] * (2 * bps)))


def _trunk(x, mods, p, past):
    bn, t, d = x.shape
    ssm_re, ssm_im, new_k, new_v = [], [], [], []
    depth = len(mods)
    for layer in range(depth):
        i = layer // 2
        sh1, sc1, g1, sh2, sc2, g2 = mods[layer]
        act_dtype = p['attn_w_o'].dtype
        if layer + 1 < depth:
            nxt = (p['norm_mix'][layer + 1], mods[layer + 1][0], mods[layer + 1][1],
                   F32 if layer % 2 else act_dtype)
        else:
            zero = jnp.zeros((bn, 1, d), F32)
            nxt = (p['norm_final'], zero, zero, F32)
        if layer % 2 == 0:
            u = h_mix if layer else _modulate(x, p['norm_mix'][layer], sh1, sc1, F32)
            if past is None:
                g, f_re, f_im = _s5_prompt(u, p['s5_bb16'][i], p['s5_cre16'][i], p['s5_cim16'][i],
                                           p['s5_pow_re'][i], p['s5_pow_im'][i], p['ssm_d'][i])
                ssm_re.append(f_re.reshape(bn, -1, SSM_STATE))
                ssm_im.append(f_im.reshape(bn, -1, SSM_STATE))
            else:
                h_re = past[3][i].reshape(t, -1)
                h_im = past[4][i].reshape(t, -1)
                g, s_re, s_im = _s5_sample(u[0], h_re, h_im, p['s5_bbar_re'][i], p['s5_bbar_im'][i],
                                           p['s5_c_re'][i], p['s5_c_im'][i],
                                           p['s5_pow_re'][i][0:1], p['s5_pow_im'][i][0:1], p['ssm_d'][i])
                g = g[None]
                ssm_re.append(s_re.reshape(t, -1, SSM_STATE))
                ssm_im.append(s_im.reshape(t, -1, SSM_STATE))
            x, h = _proj_residual(x, g, (p['ssm_w_glu_out'], p['ssm_w_glu_gate']), g1, i,
                                  nxt=(p['norm_ffn'][layer], sh2, sc2, act_dtype))
            x, h_mix = _swiglu_residual(x, h, p['ffn_w_gate'][:, None], p['ffn_w_up'][:, None],
                                        p['ffn_w_down'][:, None], g2, i, nxt=nxt)
        else:
            h = h_mix
            if past is None:
                k_t, v_t, k16, qt16, vt16, pen = _qkv_prompt(h, p['attn_w_qkv'], i)
                a = _moba_prompt(qt16, k16, vt16, pen)
                new_k.append(k_t.reshape(bn, N_HEADS, HEAD_DIM, t))
                new_v.append(v_t.reshape(bn, N_HEADS, HEAD_DIM, t))
            else:
                qkv = _matmul(h[0], p['attn_w_qkv'], i)
                q, k, v = (qkv[:, j * d:(j + 1) * d].reshape(t, N_HEADS, HEAD_DIM, 1) for j in range(3))
                a = _moba_sample(q, k, v, past[0], past[1], past[2], i).reshape(1, t, d)
                new_k.append(k.reshape(t, N_HEADS, HEAD_DIM, 1))
                new_v.append(v.reshape(t, N_HEADS, HEAD_DIM, 1))
            x = _proj_residual(x, a, (p['attn_w_o'],), g1, i)
            h, comb = _modulate(x, p['norm_ffn'][layer], sh2, sc2, act_dtype,
                                router=(p['moe_w_router'][i], p['moe_b_router'][i]))
            x, h_mix = _swiglu_residual(x, h, p['moe_w_gate'], p['moe_w_up'], p['moe_w_down'], g2, i,
                                        combine=comb, nxt=nxt)
    y = h_mix
    new_k, new_v = (jnp.stack(z).transpose(0, 1, 4, 2, 3) for z in (new_k, new_v))
    return y, jnp.stack(ssm_re), jnp.stack(ssm_im), new_k, new_v


def kernel(x_prompt, x_sample, cache_k, cache_v, page_table, state_ssm_re, state_ssm_im,
           c_prompt, c_sample, ada_w, ada_b, norm_mix, norm_ffn, norm_final,
           ssm_a_re, ssm_a_im, ssm_log_dt, ssm_b_re, ssm_b_im, ssm_c_re, ssm_c_im, ssm_d,
           ssm_w_glu_out, ssm_w_glu_gate, attn_w_qkv, attn_w_o,
           ffn_w_gate, ffn_w_up, ffn_w_down,
           moe_w_router, moe_b_router, moe_w_gate, moe_w_up, moe_w_down):
    bn, t, d = x_prompt.shape
    dec = x_sample.shape[0]
    depth = ada_w.shape[0]
    assert x_sample.shape[1] == 1 and d == N_HEADS * HEAD_DIM and t % MOBA_BLOCK == 0

    n_cond = bn + dec
    c_all = jnp.pad(jnp.concatenate([c_prompt, c_sample], axis=0), ((0, -n_cond % SUBLANES), (0, 0)))
    mod = _adaln(c_all, ada_w, ada_b)
    mods_p = [[mod[l, :bn, j * d:(j + 1) * d].reshape(bn, 1, d) for j in range(6)] for l in range(depth)]
    mods_s = [[mod[l, bn:n_cond, j * d:(j + 1) * d].reshape(1, dec, d) for j in range(6)] for l in range(depth)]

    pow_re, pow_im, bbar_re, bbar_im = _s5_prep(ssm_a_re, ssm_a_im, ssm_log_dt, ssm_b_re, ssm_b_im,
                                                S5_TILE // SUBLANES)
    c_re_bd, c_im_bd = _blockdiag_c(ssm_c_re), _blockdiag_c(ssm_c_im)
    shared = {
        'norm_mix': norm_mix, 'norm_ffn': norm_ffn, 'norm_final': norm_final, 'ssm_d': ssm_d,
        's5_pow_re': pow_re, 's5_pow_im': pow_im, 's5_bbar_re': bbar_re, 's5_bbar_im': bbar_im,
        's5_bb16': jnp.concatenate([bbar_re, bbar_im], axis=-1).astype(BF16),
        's5_c_re': c_re_bd, 's5_c_im': c_im_bd,
        's5_cre16': c_re_bd.astype(BF16), 's5_cim16': c_im_bd.astype(BF16),
        'moe_w_router': moe_w_router, 'moe_b_router': moe_b_router,
    }
    big = {
        'ssm_w_glu_out': ssm_w_glu_out, 'ssm_w_glu_gate': ssm_w_glu_gate,
        'attn_w_qkv': attn_w_qkv, 'attn_w_o': attn_w_o,
        'ffn_w_gate': ffn_w_gate, 'ffn_w_up': ffn_w_up, 'ffn_w_down': ffn_w_down,
        'moe_w_gate': moe_w_gate, 'moe_w_up': moe_w_up, 'moe_w_down': moe_w_down,
    }
    p_prompt = dict(shared, **{name: w.astype(BF16) for name, w in big.items()})
    p_sample = dict(shared, **big)

    y_p, re_p, im_p, k_p, v_p = _trunk(x_prompt, mods_p, p_prompt, None)

    past = (cache_k.transpose(0, 1, 3, 4, 2), cache_v.transpose(0, 1, 3, 4, 2),
            page_table, state_ssm_re, state_ssm_im)
    y_s, re_s, im_s, k_s, v_s = _trunk(x_sample.reshape(1, dec, d), mods_s, p_sample, past)
    return (y_p, y_s.reshape(dec, 1, d), re_p, im_p, re_s, im_s, k_p, v_p, k_s, v_s)
```

```python
import functools

import jax
import jax.numpy as jnp
from jax import lax
from jax.experimental import pallas as pl
from jax.experimental.pallas import tpu as pltpu

F32 = jnp.float32
BF16 = jnp.bfloat16
HI = lax.Precision.HIGHEST
NEG_INF = float("-inf")
LOG2E = 1.4426950408889634

SUBLANES = 8
LANES = 128

RMS_EPS = 1e-6
SSM_GROUP = 16
SSM_STATE = 64
PACK_GROUPS = LANES // SSM_GROUP
PACK_STATES = PACK_GROUPS * SSM_STATE
N_HEADS = 16
HEAD_DIM = 64
MOBA_BLOCK = 256
MOBA_TOPK = 3
TOP_K = 2

ROW_TILE = 512
S5_TILE = 256
MOBA_HEADS_PER_STEP = 8
SAMPLE_BLOCKS_PER_STEP = 4
VMEM_LIMIT = 56 * 1024 * 1024


def _cparams(sem):
    return pltpu.CompilerParams(dimension_semantics=sem, vmem_limit_bytes=VMEM_LIMIT)


def _wdot(a, w):
    if w.dtype == F32:
        return jnp.dot(a.astype(F32), w, precision=HI, preferred_element_type=F32)
    return jnp.dot(a.astype(BF16), w, preferred_element_type=F32)


def _row_tile(t):
    return ROW_TILE if t % ROW_TILE == 0 else t


def _mod_spec(arr, tm):
    d = arr.shape[-1]
    if arr.shape[1] == 1:
        return pl.BlockSpec((1, 1, d), lambda b, t: (b, 0, 0))
    return pl.BlockSpec((1, tm, d), lambda b, t: (b, t, 0))


def _adaln_kernel(c_ref, w_ref, b_ref, o_ref):
    c = c_ref[...]
    s = c * jax.nn.sigmoid(c)
    o_ref[0] = jnp.dot(s, w_ref[0], precision=HI, preferred_element_type=F32) + b_ref[0]


def _adaln(c_all, ada_w, ada_b):
    depth, d, n = ada_w.shape
    r = c_all.shape[0]
    tn = 1536
    return pl.pallas_call(
        _adaln_kernel,
        grid=(depth, n // tn),
        in_specs=[pl.BlockSpec((r, d), lambda l, j: (0, 0)),
                  pl.BlockSpec((1, d, tn), lambda l, j: (l, 0, j)),
                  pl.BlockSpec((1, 1, tn), lambda l, j: (l, 0, j))],
        out_specs=pl.BlockSpec((1, r, tn), lambda l, j: (l, 0, j)),
        out_shape=jax.ShapeDtypeStruct((depth, r, n), F32),
        compiler_params=_cparams(("arbitrary", "arbitrary")),
        name="adaln",
    )(c_all, ada_w, ada_b.reshape(depth, 1, n))


def _rms(x, g):
    return x * lax.rsqrt(jnp.mean(x * x, axis=-1, keepdims=True) + RMS_EPS) * g


def _modulate_kernel(x_ref, g_ref, sh_ref, sc_ref, o_ref):
    h = _rms(x_ref[0], g_ref[...]) * (1.0 + sc_ref[0]) + sh_ref[0]
    o_ref[0] = h.astype(o_ref.dtype)


def _route_top2(h, wr_ref, br_ref):
    logits = jnp.dot(h, wr_ref[...], precision=HI, preferred_element_type=F32) + br_ref[...]
    lane = lax.broadcasted_iota(jnp.int32, logits.shape, 1)
    m1 = jnp.max(logits, axis=-1, keepdims=True)
    i1 = jnp.min(jnp.where(logits == m1, lane, LANES), axis=-1, keepdims=True)
    rest = jnp.where(lane == i1, NEG_INF, logits)
    m2 = jnp.max(rest, axis=-1, keepdims=True)
    i2 = jnp.min(jnp.where(rest == m2, lane, LANES), axis=-1, keepdims=True)
    e2 = jnp.exp(m2 - m1)
    den = 1.0 + e2
    return jnp.where(lane == i1, 1.0 / den, 0.0) + jnp.where(lane == i2, e2 / den, 0.0)


def _modulate(x, g, shift, scale, out_dtype):
    bn, t, d = x.shape
    tm = _row_tile(t)
    x_spec = pl.BlockSpec((1, tm, d), lambda b, i: (b, i, 0))
    return pl.pallas_call(
        _modulate_kernel, grid=(bn, t // tm),
        in_specs=[x_spec, pl.BlockSpec((1, d), lambda b, i: (0, 0)), _mod_spec(shift, tm), _mod_spec(scale, tm)],
        out_specs=x_spec, out_shape=jax.ShapeDtypeStruct((bn, t, d), out_dtype),
        compiler_params=_cparams(("arbitrary", "arbitrary")), name="modulate",
    )(x, g.reshape(1, d), shift, scale)


def _store_residual(x_new, out_refs):
    if len(out_refs) == 1:
        out_refs[0][0] = x_new
        return
    g_ref, sh_ref, sc_ref = out_refs[:3]
    h = _rms(x_new, g_ref[...]) * (1.0 + sc_ref[0]) + sh_ref[0]
    if len(out_refs) == 5:
        o_ref, h_ref = out_refs[3:]
    else:
        wr_ref, br_ref, o_ref, h_ref, comb_ref = out_refs[3:]
        comb_ref[0] = _route_top2(h, wr_ref, br_ref)
    o_ref[0] = x_new
    h_ref[0] = h.astype(h_ref.dtype)


def _proj_residual_kernel(x_ref, a_ref, w_ref, gate_ref, *out_refs):
    y = _wdot(a_ref[0], w_ref[0])
    _store_residual(x_ref[0] + gate_ref[0] * y, out_refs)


def _glu_residual_kernel(x_ref, a_ref, wo_ref, wg_ref, gate_ref, *out_refs):
    a = a_ref[0]
    y = _wdot(a, wo_ref[0])
    z = _wdot(a, wg_ref[0])
    _store_residual(x_ref[0] + gate_ref[0] * (y * jax.nn.sigmoid(z)), out_refs)


def _next_modulate_io(nxt, d, tm, x_spec, shape, const_map, mod_spec):
    if nxt is None:
        return [], [], x_spec, jax.ShapeDtypeStruct(shape, F32)
    g, shift, scale, dtype = nxt[:4]
    args = [g.reshape(1, d), shift, scale]
    specs = [pl.BlockSpec((1, d), const_map), mod_spec(shift), mod_spec(scale)]
    out_specs = [x_spec, x_spec]
    out_shape = [jax.ShapeDtypeStruct(shape, F32), jax.ShapeDtypeStruct(shape, dtype)]
    if len(nxt) == 6:
        w_router, b_router = nxt[4:]
        n_exp = w_router.shape[1]
        args += [jnp.pad(w_router, ((0, 0), (0, LANES - n_exp))),
                 jnp.pad(b_router.reshape(1, n_exp), ((0, 0), (0, LANES - n_exp)), constant_values=NEG_INF)]
        specs += [pl.BlockSpec((d, LANES), const_map), pl.BlockSpec((1, LANES), const_map)]
        out_specs.append(pl.BlockSpec(x_spec.block_shape[:2] + (LANES,), x_spec.index_map))
        out_shape.append(jax.ShapeDtypeStruct(shape[:2] + (LANES,), F32))
    return args, specs, out_specs, out_shape


def _proj_residual(x, a, weights, gate, layer, nxt=None):
    bn, t, d = x.shape
    tm = _row_tile(t)
    x_spec = pl.BlockSpec((1, tm, d), lambda b, i: (b, i, 0))
    w_spec = pl.BlockSpec((1, d, d), lambda b, i: (layer, 0, 0))
    kern = _proj_residual_kernel if len(weights) == 1 else _glu_residual_kernel
    n_args, n_specs, out_specs, out_shape = _next_modulate_io(
        nxt, d, tm, x_spec, (bn, t, d), lambda b, i: (0, 0), lambda m: _mod_spec(m, tm))
    return pl.pallas_call(
        kern, grid=(bn, t // tm),
        in_specs=[x_spec, x_spec] + [w_spec] * len(weights) + [_mod_spec(gate, tm)] + n_specs,
        out_specs=out_specs, out_shape=out_shape,
        compiler_params=_cparams(("arbitrary", "arbitrary")), name="proj_residual",
    )(x, a, *weights, gate, *n_args)


def _swiglu_kernel(x_ref, h_ref, wg_ref, wu_ref, wd_ref, gate_ref, *rest):
    out_refs, acc_ref = rest[:-1], rest[-1]
    e, f = pl.program_id(2), pl.program_id(3)

    @pl.when((e == 0) & (f == 0))
    def _():
        acc_ref[...] = jnp.zeros_like(acc_ref)

    h = h_ref[0]
    a = _wdot(h, wg_ref[0, 0])
    u = _wdot(h, wu_ref[0, 0])
    acc_ref[...] += _wdot(a * jax.nn.sigmoid(a) * u, wd_ref[0, 0])

    @pl.when((e == pl.num_programs(2) - 1) & (f == pl.num_programs(3) - 1))
    def _():
        _store_residual(x_ref[0] + gate_ref[0] * acc_ref[...], out_refs)


def _moe_swiglu_kernel(x_ref, h_ref, comb_ref, wg_ref, wu_ref, wd_ref, gate_ref, *rest):
    out_refs, acc_ref = rest[:-1], rest[-1]
    e, f = pl.program_id(2), pl.program_id(3)

    @pl.when((e == 0) & (f == 0))
    def _():
        acc_ref[...] = jnp.zeros_like(acc_ref)

    h = h_ref[0]
    a = _wdot(h, wg_ref[0, 0])
    u = _wdot(h, wu_ref[0, 0])
    act = a * jax.nn.sigmoid(a) * u
    comb = comb_ref[0]
    lane = lax.broadcasted_iota(jnp.int32, comb.shape, 1)
    w_e = jnp.sum(jnp.where(lane == e, comb, 0.0), axis=-1, keepdims=True)
    acc_ref[...] += w_e * _wdot(act, wd_ref[0, 0])

    @pl.when((e == pl.num_programs(2) - 1) & (f == pl.num_programs(3) - 1))
    def _():
        _store_residual(x_ref[0] + gate_ref[0] * acc_ref[...], out_refs)


def _swiglu_residual(x, h, w_gate, w_up, w_down, gate, layer, combine=None, nxt=None):
    bn, t, d = x.shape
    _, n_exp, _, ff = w_gate.shape
    tm = _row_tile(t)
    tf = 1408
    assert ff % tf == 0
    x_spec = pl.BlockSpec((1, tm, d), lambda b, i, e, f: (b, i, 0))
    wi_spec = pl.BlockSpec((1, 1, d, tf), lambda b, i, e, f: (layer, e, 0, f))
    wd_spec = pl.BlockSpec((1, 1, tf, d), lambda b, i, e, f: (layer, e, f, 0))
    def mod_spec(arr):
        if arr.shape[1] == 1:
            return pl.BlockSpec((1, 1, d), lambda b, i, e, f: (b, 0, 0))
        return pl.BlockSpec((1, tm, d), lambda b, i, e, f: (b, i, 0))

    n_args, n_specs, out_specs, out_shape = _next_modulate_io(
        nxt, d, tm, x_spec, (bn, t, d), lambda b, i, e, f: (0, 0), mod_spec)
    if combine is None:
        kern, in_specs, args = _swiglu_kernel, [x_spec, x_spec], [x, h]
    else:
        kern = _moe_swiglu_kernel
        in_specs = [x_spec, x_spec, pl.BlockSpec((1, tm, LANES), lambda b, i, e, f: (b, i, 0))]
        args = [x, h, combine]
    return pl.pallas_call(
        kern, grid=(bn, t // tm, n_exp, ff // tf),
        in_specs=in_specs + [wi_spec, wi_spec, wd_spec, mod_spec(gate)] + n_specs,
        out_specs=out_specs, out_shape=out_shape,
        scratch_shapes=[pltpu.VMEM((tm, d), F32)],
        compiler_params=_cparams(("arbitrary",) * 4), name="swiglu",
    )(*args, w_gate, w_up, w_down, gate, *n_args)


def _s5_prep_kernel(are_ref, aim_ref, ldt_ref, bre_ref, bim_ref,
                    pre_ref, pim_ref, obr_ref, obi_ref, *, seg):
    a_re, a_im = are_ref[0], aim_ref[0]
    dt = jnp.exp(ldt_ref[0])
    steps = lax.broadcasted_iota(jnp.int32, (seg, PACK_STATES), 0).astype(F32) + 1.0
    mag = jnp.exp(steps * (dt * a_re))
    ang = steps * (dt * a_im)
    pre_ref[0] = mag * jnp.cos(ang)
    pim_ref[0] = mag * jnp.sin(ang)
    mag1 = jnp.exp(dt * a_re)
    ab_re = mag1 * jnp.cos(dt * a_im)
    ab_im = mag1 * jnp.sin(dt * a_im)
    den = a_re * a_re + a_im * a_im
    num_re = ab_re - 1.0
    coef_re = (num_re * a_re + ab_im * a_im) / den
    coef_im = (ab_im * a_re - num_re * a_im) / den
    b_re, b_im = bre_ref[0, 0], bim_ref[0, 0]
    obr_ref[0, 0] = coef_re * b_re - coef_im * b_im
    obi_ref[0, 0] = coef_re * b_im + coef_im * b_re


def _blockdiag_b(b):
    nl, g, p, j = b.shape
    b5 = b.reshape(nl, g // PACK_GROUPS, PACK_GROUPS, p, j).transpose(0, 1, 2, 4, 3)
    eye = jnp.eye(PACK_GROUPS, dtype=b.dtype)
    out = b5[:, :, :, :, None, :] * eye[None, None, :, None, :, None]
    return out.reshape(nl, g // PACK_GROUPS, PACK_GROUPS * j, PACK_GROUPS * p)


def _blockdiag_c(c):
    nl, g, j, p = c.shape
    c5 = c.reshape(nl, g // PACK_GROUPS, PACK_GROUPS, j, p).transpose(0, 1, 2, 4, 3)
    eye = jnp.eye(PACK_GROUPS, dtype=c.dtype)
    out = c5[:, :, :, :, None, :] * eye[None, None, :, None, :, None]
    return out.reshape(nl, g // PACK_GROUPS, PACK_GROUPS * p, PACK_GROUPS * j)


def _s5_prep(a_re, a_im, log_dt, b_re, b_im, seg):
    nl, g, p = a_re.shape
    n_state = g * p
    n_pack = g // PACK_GROUPS
    row = lambda a: a.reshape(nl, 1, n_state)
    ldt = jnp.broadcast_to(log_dt[:, :, None], (nl, g, p))
    braw_re, braw_im = _blockdiag_b(b_re), _blockdiag_b(b_im)
    vec_spec = pl.BlockSpec((1, 1, PACK_STATES), lambda l, k: (l, 0, k))
    mat_spec = pl.BlockSpec((1, 1, LANES, PACK_STATES), lambda l, k: (l, k, 0, 0))
    pow_spec = pl.BlockSpec((1, seg, PACK_STATES), lambda l, k: (l, 0, k))
    return pl.pallas_call(
        functools.partial(_s5_prep_kernel, seg=seg),
        grid=(nl, n_pack),
        in_specs=[vec_spec, vec_spec, vec_spec, mat_spec, mat_spec],
        out_specs=[pow_spec, pow_spec, mat_spec, mat_spec],
        out_shape=[jax.ShapeDtypeStruct((nl, seg, n_state), F32)] * 2
        + [jax.ShapeDtypeStruct(braw_re.shape, F32)] * 2,
        compiler_params=_cparams(("arbitrary", "arbitrary")), name="s5_prep",
    )(row(a_re), row(a_im), row(ldt), braw_re, braw_im)


def _s5_prompt_kernel(u_ref, bb_ref, cre_ref, cim_ref, pre_ref, pim_ref, d_ref,
                      g_ref, fre_ref, fim_ref,
                      uperm, sre, sim, gperm, car_re, car_im, ini_re, ini_im, *, seg, chunk):
    n_pack = bb_ref.shape[0]
    n_state = sre.shape[1]

    @pl.when(pl.program_id(1) == 0)
    def _():
        car_re[...] = jnp.zeros_like(car_re)
        car_im[...] = jnp.zeros_like(car_im)

    for pk in range(n_pack):
        for k in range(SUBLANES):
            uperm[pk, pl.ds(k, seg, stride=SUBLANES), :] = (
                u_ref[0, k * seg:(k + 1) * seg, pk * LANES:(pk + 1) * LANES])

    for pk in range(n_pack):
        z = jnp.dot(uperm[pk].astype(BF16), bb_ref[pk], preferred_element_type=F32)
        sre[:, pk * PACK_STATES:(pk + 1) * PACK_STATES] = z[:, :PACK_STATES]
        sim[:, pk * PACK_STATES:(pk + 1) * PACK_STATES] = z[:, PACK_STATES:]

    for c in range(n_state // chunk):
        cs = slice(c * chunk, (c + 1) * chunk)
        lam_re = jnp.broadcast_to(pre_ref[0:1, cs], (SUBLANES, chunk))
        lam_im = jnp.broadcast_to(pim_ref[0:1, cs], (SUBLANES, chunk))

        def scan_step(n, carry, cs=cs, lam_re=lam_re, lam_im=lam_im):
            h_re, h_im = carry
            rows = pl.ds(pl.multiple_of(n * SUBLANES, SUBLANES), SUBLANES)
            n_re = lam_re * h_re - lam_im * h_im + sre[rows, cs]
            n_im = lam_re * h_im + lam_im * h_re + sim[rows, cs]
            sre[rows, cs] = n_re
            sim[rows, cs] = n_im
            return n_re, n_im

        zero = jnp.zeros((SUBLANES, chunk), F32)
        end_re, end_im = lax.fori_loop(0, seg, scan_step, (zero, zero), unroll=4)

        seg_re, seg_im = pre_ref[seg - 1:seg, cs], pim_ref[seg - 1:seg, cs]
        cur_re, cur_im = car_re[:, cs], car_im[:, cs]
        for k in range(SUBLANES):
            ini_re[k:k + 1, cs] = cur_re
            ini_im[k:k + 1, cs] = cur_im
            nxt_re = end_re[k:k + 1] + seg_re * cur_re - seg_im * cur_im
            nxt_im = end_im[k:k + 1] + seg_re * cur_im + seg_im * cur_re
            cur_re, cur_im = nxt_re, nxt_im
        car_re[:, cs] = cur_re
        car_im[:, cs] = cur_im

        in_re, in_im = ini_re[:, cs], ini_im[:, cs]

        def fix_step(n, carry, cs=cs, in_re=in_re, in_im=in_im):
            rows = pl.ds(pl.multiple_of(n * SUBLANES, SUBLANES), SUBLANES)
            p_re, p_im = pre_ref[pl.ds(n, 1), cs], pim_ref[pl.ds(n, 1), cs]
            sre[rows, cs] = sre[rows, cs] + p_re * in_re - p_im * in_im
            sim[rows, cs] = sim[rows, cs] + p_re * in_im + p_im * in_re
            return carry

        lax.fori_loop(0, seg, fix_step, 0, unroll=4)

    fre_ref[0] = car_re[...]
    fim_ref[0] = car_im[...]

    for pk in range(n_pack):
        ss = slice(pk * PACK_STATES, (pk + 1) * PACK_STATES)
        ls = slice(pk * LANES, (pk + 1) * LANES)
        y = (jnp.dot(sre[:, ss].astype(BF16), cre_ref[pk], preferred_element_type=F32)
             - jnp.dot(sim[:, ss].astype(BF16), cim_ref[pk], preferred_element_type=F32))
        y = y + d_ref[:, ls] * uperm[pk]
        gperm[pk] = jax.nn.gelu(y, approximate=True)

    for pk in range(n_pack):
        for k in range(SUBLANES):
            g_ref[0, k * seg:(k + 1) * seg, pk * LANES:(pk + 1) * LANES] = (
                gperm[pk, pl.ds(k, seg, stride=SUBLANES), :].astype(g_ref.dtype))


def _s5_prompt(u, bb16, cre16, cim16, pow_re, pow_im, d_skip):
    bn, t, d = u.shape
    n_pack = bb16.shape[0]
    n_state = pow_re.shape[1]
    tt = S5_TILE
    seg = tt // SUBLANES
    assert t % tt == 0 and pow_re.shape[0] == seg
    const = lambda shape: pl.BlockSpec(shape, lambda b, i: (0,) * len(shape))
    u_spec = pl.BlockSpec((1, tt, d), lambda b, i: (b, i, 0))
    fin_spec = pl.BlockSpec((1, 1, n_state), lambda b, i: (b, 0, 0))
    return pl.pallas_call(
        functools.partial(_s5_prompt_kernel, seg=seg, chunk=512),
        grid=(bn, t // tt),
        in_specs=[u_spec, const(bb16.shape), const(cre16.shape), const(cim16.shape),
                  const(pow_re.shape), const(pow_im.shape), const((1, d))],
        out_specs=[u_spec, fin_spec, fin_spec],
        out_shape=[jax.ShapeDtypeStruct((bn, t, d), BF16),
                   jax.ShapeDtypeStruct((bn, 1, n_state), F32),
                   jax.ShapeDtypeStruct((bn, 1, n_state), F32)],
        scratch_shapes=[pltpu.VMEM((n_pack, tt, LANES), F32), pltpu.VMEM((tt, n_state), F32),
                        pltpu.VMEM((tt, n_state), F32), pltpu.VMEM((n_pack, tt, LANES), F32),
                        pltpu.VMEM((1, n_state), F32), pltpu.VMEM((1, n_state), F32),
                        pltpu.VMEM((SUBLANES, n_state), F32), pltpu.VMEM((SUBLANES, n_state), F32)],
        compiler_params=_cparams(("arbitrary", "arbitrary")), name="s5_prompt",
    )(u, bb16, cre16, cim16, pow_re, pow_im, d_skip.reshape(1, d))


def _s5_sample_kernel(u_ref, hre_ref, him_ref, bre_ref, bim_ref, cre_ref, cim_ref,
                      lre_ref, lim_ref, d_ref, g_ref, sre_ref, sim_ref):
    for pk in range(bre_ref.shape[0]):
        ss = slice(pk * PACK_STATES, (pk + 1) * PACK_STATES)
        ls = slice(pk * LANES, (pk + 1) * LANES)
        u = u_ref[:, ls]
        lam_re, lam_im = lre_ref[:, ss], lim_ref[:, ss]
        h_re, h_im = hre_ref[:, ss], him_ref[:, ss]
        s_re = (jnp.dot(u, bre_ref[pk], precision=HI, preferred_element_type=F32)
                + (lam_re * h_re - lam_im * h_im))
        s_im = (jnp.dot(u, bim_ref[pk], precision=HI, preferred_element_type=F32)
                + (lam_re * h_im + lam_im * h_re))
        sre_ref[:, ss] = s_re
        sim_ref[:, ss] = s_im
        y = (jnp.dot(s_re, cre_ref[pk], precision=HI, preferred_element_type=F32)
             - jnp.dot(s_im, cim_ref[pk], precision=HI, preferred_element_type=F32))
        y = y + d_ref[:, ls] * u
        g_ref[:, ls] = jax.nn.gelu(y, approximate=True).astype(g_ref.dtype)


def _s5_sample(u, h_re, h_im, bbar_re, bbar_im, c_re, c_im, lam_re, lam_im, d_skip):
    rows, d = u.shape
    n_state = h_re.shape[1]
    return pl.pallas_call(
        _s5_sample_kernel,
        out_shape=[jax.ShapeDtypeStruct((rows, d), F32),
                   jax.ShapeDtypeStruct((rows, n_state), F32),
                   jax.ShapeDtypeStruct((rows, n_state), F32)],
        compiler_params=pltpu.CompilerParams(vmem_limit_bytes=VMEM_LIMIT), name="s5_sample",
    )(u, h_re, h_im, bbar_re, bbar_im, c_re, c_im, lam_re, lam_im, d_skip.reshape(1, d))


def _top_blocks_penalty(gate, blk, n_valid_below):
    gate = jnp.where(blk < n_valid_below, gate, NEG_INF)
    n_blk = gate.shape[0]
    sel = jnp.zeros(gate.shape, F32)
    for _ in range(MOBA_TOPK):
        m = jnp.max(gate, axis=0, keepdims=True)
        idx = jnp.min(jnp.where(gate == m, blk, n_blk), axis=0, keepdims=True)
        hit = blk == idx
        sel = jnp.where(hit & (m > NEG_INF), 1.0, sel)
        gate = jnp.where(hit, NEG_INF, gate)
    return jnp.where(sel > 0.0, 0.0, NEG_INF)


def _qkv_prompt_kernel(h_ref, w_ref, kt_ref, vt_ref, k16_ref, qt16_ref, vt16_ref, pen_ref, kmean):
    i = pl.program_id(1)
    d = k16_ref.shape[-1]
    n_blk = kmean.shape[0]

    @pl.when(i == 0)
    def _():
        kmean[...] = jnp.zeros_like(kmean)

    r = jnp.dot(h_ref[0], w_ref[0], preferred_element_type=F32)
    q, k, v = r[:, :d], r[:, d:2 * d], r[:, 2 * d:]
    k16_ref[0] = k.astype(BF16)
    q_t, v_t = q.T, v.T
    kt_ref[0] = k.T
    vt_ref[0] = v_t
    qt16_ref[0] = (q_t * (LOG2E * HEAD_DIM ** -0.5)).astype(BF16)
    vt16_ref[0, 0] = v_t.astype(BF16)

    km = kmean[...]
    blk = lax.broadcasted_iota(jnp.int32, (n_blk, MOBA_BLOCK), 0)
    for h in range(N_HEADS):
        hs = slice(h * HEAD_DIM, (h + 1) * HEAD_DIM)
        gate = jnp.dot(km[:, hs], q_t[hs, :], precision=HI, preferred_element_type=F32)
        pen_ref[0, h] = _top_blocks_penalty(gate, blk, i)

    row = lax.broadcasted_iota(jnp.int32, km.shape, 0)
    kmean[...] = jnp.where(row == i, jnp.mean(k, axis=0, keepdims=True), km)


def _qkv_prompt(h16, w16, layer):
    bn, t, d = h16.shape
    n_blk = t // MOBA_BLOCK
    tile = pl.BlockSpec((1, MOBA_BLOCK, d), lambda b, i: (b, i, 0))
    tile_t = pl.BlockSpec((1, d, MOBA_BLOCK), lambda b, i: (b, 0, i))
    return pl.pallas_call(
        _qkv_prompt_kernel, grid=(bn, n_blk),
        in_specs=[tile, pl.BlockSpec((1,) + w16.shape[1:], lambda b, i: (layer, 0, 0))],
        out_specs=[tile_t, tile_t, tile, tile_t,
                   pl.BlockSpec((1, 1, d, MOBA_BLOCK), lambda b, i: (b, i, 0, 0)),
                   pl.BlockSpec((1, N_HEADS, n_blk, MOBA_BLOCK), lambda b, i: (b, 0, 0, i))],
        out_shape=[jax.ShapeDtypeStruct((bn, d, t), F32), jax.ShapeDtypeStruct((bn, d, t), F32),
                   jax.ShapeDtypeStruct((bn, t, d), BF16),
                   jax.ShapeDtypeStruct((bn, d, t), BF16),
                   jax.ShapeDtypeStruct((bn, n_blk, d, MOBA_BLOCK), BF16),
                   jax.ShapeDtypeStruct((bn, N_HEADS, n_blk, t), F32)],
        scratch_shapes=[pltpu.VMEM((n_blk, d), F32)],
        compiler_params=_cparams(("arbitrary", "arbitrary")), name="qkv_prompt",
    )(h16, w16)


def _moba_prompt_kernel(slope_ref, qt_ref, k_ref, vt_ref, pen_ref, o_ref, s_ring, *, n_heads):
    hp, i = pl.program_id(1), pl.program_id(2)
    blk = MOBA_BLOCK
    k_iota = lax.broadcasted_iota(jnp.int32, (blk, blk), 0)
    q_iota = lax.broadcasted_iota(jnp.int32, (blk, blk), 1)
    causal = k_iota <= q_iota
    k_local = k_iota.astype(F32)
    head_row = lax.broadcasted_iota(jnp.int32, (LANES, blk), 0) // HEAD_DIM
    heads = range(n_heads)
    slab = [slice((a // 2) * LANES, (a // 2 + 1) * LANES) for a in heads]
    q_h = []
    for a in heads:
        q_slab = qt_ref[0, slab[a], :]
        q_h.append(jnp.where(head_row == a % 2, q_slab, jnp.zeros_like(q_slab)))
    slope = [slope_ref[n_heads * hp + a] for a in heads]
    k_bias = [slope[a] * k_local for a in heads]
    rows = [slice(a * HEAD_DIM, (a + 1) * HEAD_DIM) for a in heads]

    def scores(a, n):
        k_n = k_ref[0, pl.ds(pl.multiple_of(n * blk, blk), blk), slab[a]]
        return jnp.dot(k_n, q_h[a], preferred_element_type=F32) + k_bias[a]

    def offset(a, n):
        shift = slope[a] * jnp.full((1, blk), (n - i) * blk, jnp.int32).astype(F32)
        return shift + pen_ref[0, a, pl.ds(n, 1), :]

    ones_rows = jnp.ones((2 * SUBLANES, blk), BF16)

    def weighted_values(a, n, p):
        v_aug = jnp.concatenate([vt_ref[0, n, rows[a], :], ones_rows], axis=0)
        return jnp.dot(v_aug, p.astype(BF16), preferred_element_type=F32)

    init = []
    for a in heads:
        s = jnp.where(causal, scores(a, i), NEG_INF)
        m0 = jnp.max(s, axis=0, keepdims=True)
        init += [m0, weighted_values(a, i, jnp.exp2(s - m0))]

    def update(carry, blocks, get_scores):
        out = []
        for a in heads:
            m, acc = carry[2 * a:2 * a + 2]
            off_blk = [offset(a, n) for n in blocks]
            m_new = m
            for k, off in enumerate(off_blk):
                m_new = jnp.maximum(m_new, jnp.max(get_scores(a, k), axis=0, keepdims=True) + off)
            acc = jnp.exp2(m - m_new) * acc
            for k, (n, off) in enumerate(zip(blocks, off_blk)):
                acc = acc + weighted_values(a, n, jnp.exp2(get_scores(a, k) - (m_new - off)))
            out += [m_new, acc]
        return tuple(out)

    n_pairs = i // 2

    def store_pair_scores(slot, j):
        for k, n in enumerate((jnp.minimum(2 * j, i), jnp.minimum(2 * j + 1, i))):
            for a in heads:
                s_ring[slot, 2 * a + k] = scores(a, n)

    def pair_step(j, state):
        slot = j & 1
        state = update(state, (2 * j, 2 * j + 1), lambda a, k: s_ring[slot, 2 * a + k])
        store_pair_scores(1 - slot, j + 1)
        return state

    store_pair_scores(0, 0)
    carry = lax.fori_loop(0, n_pairs, pair_step, tuple(init))

    def single_step(n, state):
        tiles = [scores(a, n) for a in heads]
        return update(state, (n,), lambda a, k: tiles[a])

    carry = lax.fori_loop(2 * n_pairs, i, single_step, carry)
    outs = [carry[2 * a + 1][:HEAD_DIM] / carry[2 * a + 1][HEAD_DIM:HEAD_DIM + 1] for a in heads]
    o_ref[0] = jnp.concatenate(outs, axis=0).T.astype(o_ref.dtype)


def _alibi_slopes():
    return 2.0 ** (-8.0 * jnp.arange(1, N_HEADS + 1, dtype=F32) / N_HEADS)


def _moba_prompt(qt16, k16, vt16, pen):
    bn, d, t = qt16.shape
    n_blk = t // MOBA_BLOCK
    hps = MOBA_HEADS_PER_STEP
    width = hps * HEAD_DIM
    return pl.pallas_call(
        functools.partial(_moba_prompt_kernel, n_heads=hps),
        grid_spec=pltpu.PrefetchScalarGridSpec(
            num_scalar_prefetch=0,
            grid=(bn, N_HEADS // hps, n_blk),
            in_specs=[pl.BlockSpec(memory_space=pltpu.SMEM),
                      pl.BlockSpec((1, width, MOBA_BLOCK), lambda b, hp, i: (b, hp, i)),
                      pl.BlockSpec((1, t, width), lambda b, hp, i: (b, 0, hp)),
                      pl.BlockSpec((1, n_blk, width, MOBA_BLOCK), lambda b, hp, i: (b, 0, hp, 0)),
                      pl.BlockSpec((1, hps, n_blk, MOBA_BLOCK), lambda b, hp, i: (b, hp, 0, i))],
            out_specs=pl.BlockSpec((1, MOBA_BLOCK, width), lambda b, hp, i: (b, i, hp)),
            scratch_shapes=[pltpu.VMEM((2, 2 * hps, MOBA_BLOCK, MOBA_BLOCK), F32)],
        ),
        out_shape=jax.ShapeDtypeStruct((bn, t, d), BF16),
        compiler_params=_cparams(("arbitrary",) * 3), name="moba_prompt",
    )(_alibi_slopes() * LOG2E, qt16, k16, vt16, pen)


def _matmul_kernel(a_ref, w_ref, o_ref):
    o_ref[...] = _wdot(a_ref[...], w_ref[0])


def _matmul(a, w, layer):
    rows, cols = a.shape[0], w.shape[2]
    return pl.pallas_call(
        _matmul_kernel, grid=(1,),
        in_specs=[pl.BlockSpec(a.shape, lambda g: (0, 0)),
                  pl.BlockSpec((1,) + w.shape[1:], lambda g: (layer, 0, 0))],
        out_specs=pl.BlockSpec((rows, cols), lambda g: (0, 0)),
        out_shape=jax.ShapeDtypeStruct((rows, cols), F32),
        compiler_params=_cparams(("arbitrary",)), name="matmul",
    )(a, w)


def _moba_sample_kernel(pt_ref, q_ref, kn_ref, vn_ref, slope_ref, *refs, past_len, blocks_per_step):
    del pt_ref
    n_pages = 2 * blocks_per_step
    k_pages, v_pages = refs[:n_pages], refs[n_pages:2 * n_pages]
    o_ref, q_lanes, m_s, l_s, g_s, o_s = refs[2 * n_pages:]
    step = pl.program_id(1)
    n_blk = pl.num_programs(1) * blocks_per_step
    page = k_pages[0].shape[-1]
    q_col = q_ref[0]
    blk_lane = lax.broadcasted_iota(jnp.int32, m_s.shape, 2)
    blk_lane_o = lax.broadcasted_iota(jnp.int32, o_s.shape, 2)

    @pl.when(step == 0)
    def _():
        q_lanes[...] = jnp.broadcast_to(q_col, q_lanes.shape)
        m_s[...] = jnp.zeros_like(m_s)
        l_s[...] = jnp.zeros_like(l_s)
        g_s[...] = jnp.zeros_like(g_s)
        o_s[...] = jnp.zeros_like(o_s)

    q_b = q_lanes[...]
    slope = slope_ref[...]
    tok = lax.broadcasted_iota(jnp.int32, (1, 1, page), 2)
    for j in range(blocks_per_step):
        n = step * blocks_per_step + j
        raw = [jnp.sum(k_pages[2 * j + half][0, 0] * q_b, axis=1, keepdims=True) for half in range(2)]
        s_pages = []
        for half, r in enumerate(raw):
            dist = (past_len - (n * MOBA_BLOCK + half * page) - tok).astype(F32)
            s_pages.append(r * (HEAD_DIM ** -0.5) - slope * dist)
        m = jnp.maximum(jnp.max(s_pages[0], axis=-1, keepdims=True),
                        jnp.max(s_pages[1], axis=-1, keepdims=True))
        l = jnp.zeros(m.shape, F32)
        pv = jnp.zeros(q_b.shape, F32)
        for half, s in enumerate(s_pages):
            p = jnp.exp(s - m)
            l = l + jnp.sum(p, axis=-1, keepdims=True)
            pv = pv + p * v_pages[2 * j + half][0, 0]
        gate = (jnp.sum(raw[0], axis=-1, keepdims=True)
                + jnp.sum(raw[1], axis=-1, keepdims=True)) * (1.0 / MOBA_BLOCK)
        m_s[...] = jnp.where(blk_lane == n, m, m_s[...])
        l_s[...] = jnp.where(blk_lane == n, l, l_s[...])
        g_s[...] = jnp.where(blk_lane == n, gate, g_s[...])
        o_s[...] = jnp.where(blk_lane_o == n, jnp.sum(pv, axis=-1, keepdims=True), o_s[...])

    @pl.when(step == pl.num_programs(1) - 1)
    def _():
        gates = jnp.where(blk_lane < n_blk, g_s[...], NEG_INF)
        sel = jnp.zeros(gates.shape, F32)
        for _ in range(MOBA_TOPK):
            g_max = jnp.max(gates, axis=-1, keepdims=True)
            idx = jnp.min(jnp.where(gates == g_max, blk_lane, LANES), axis=-1, keepdims=True)
            hit = blk_lane == idx
            sel = jnp.where(hit & (g_max > NEG_INF), 1.0, sel)
            gates = jnp.where(hit, NEG_INF, gates)
        chosen = sel > 0.0
        s_own = jnp.sum(q_col * kn_ref[0], axis=1, keepdims=True) * (HEAD_DIM ** -0.5)
        m_all = m_s[...]
        m_tot = jnp.maximum(jnp.max(jnp.where(chosen, m_all, NEG_INF), axis=-1, keepdims=True), s_own)
        w = jnp.where(chosen, jnp.exp(m_all - m_tot), 0.0)
        w_own = jnp.exp(s_own - m_tot)
        l_tot = jnp.sum(w * l_s[...], axis=-1, keepdims=True) + w_own
        o_tot = jnp.sum(w * o_s[...], axis=-1, keepdims=True) + w_own * vn_ref[0]
        o_ref[0] = o_tot / l_tot


def _moba_sample(q, k_new, v_new, cache_k, cache_v, page_table, layer):
    dec, n_heads, hd, _ = q.shape
    n_pages = page_table.shape[1]
    page = cache_k.shape[-1]
    assert MOBA_BLOCK == 2 * page
    n_blk = n_pages // 2
    bps = SAMPLE_BLOCKS_PER_STEP if n_blk % SAMPLE_BLOCKS_PER_STEP == 0 else 1
    past_len = n_pages * page
    vec = pl.BlockSpec((1, n_heads, hd, 1), lambda b, n, pt: (b, 0, 0, 0))

    def page_spec(idx):
        return pl.BlockSpec((1, 1, n_heads, hd, page),
                            lambda b, n, pt: (layer, pt[b, 2 * bps * n + idx], 0, 0, 0))

    page_specs = [page_spec(idx) for idx in range(2 * bps)]
    assert n_blk <= LANES
    stat = pltpu.VMEM((n_heads, 1, LANES), F32)
    return pl.pallas_call(
        functools.partial(_moba_sample_kernel, past_len=past_len, blocks_per_step=bps),
        grid_spec=pltpu.PrefetchScalarGridSpec(
            num_scalar_prefetch=1,
            grid=(dec, n_blk // bps),
            in_specs=[vec, vec, vec, pl.BlockSpec((n_heads, 1, 1), lambda b, n, pt: (0, 0, 0))]
            + page_specs + page_specs,
            out_specs=vec,
            scratch_shapes=[pltpu.VMEM((n_heads, hd, page), F32), stat, stat, stat,
                            pltpu.VMEM((n_heads, hd, LANES), F32)],
        ),
        out_shape=jax.ShapeDtypeStruct((dec, n_heads, hd, 1), F32),
        compiler_params=_cparams(("arbitrary", "arbitrary")), name="moba_sample",
    )(page_table, q, k_new, v_new, _alibi_slopes().reshape(n_heads, 1, 1),
      *([cache_k] * (2 * bps)), *([cache_v] * (2 * bps)))


def _trunk(x, mods, p, past):
    bn, t, d = x.shape
    ssm_re, ssm_im, new_k, new_v = [], [], [], []
    depth = len(mods)
    for layer in range(depth):
        i = layer // 2
        sh1, sc1, g1, sh2, sc2, g2 = mods[layer]
        act_dtype = p['attn_w_o'].dtype
        if layer + 1 < depth:
            nxt = (p['norm_mix'][layer + 1], mods[layer + 1][0], mods[layer + 1][1],
                   F32 if layer % 2 else act_dtype)
        else:
            zero = jnp.zeros((bn, 1, d), F32)
            nxt = (p['norm_final'], zero, zero, F32)
        if layer % 2 == 0:
            u = h_mix if layer else _modulate(x, p['norm_mix'][layer], sh1, sc1, F32)
            if past is None:
                g, f_re, f_im = _s5_prompt(u, p['s5_bb16'][i], p['s5_cre16'][i], p['s5_cim16'][i],
                                           p['s5_pow_re'][i], p['s5_pow_im'][i], p['ssm_d'][i])
                ssm_re.append(f_re.reshape(bn, -1, SSM_STATE))
                ssm_im.append(f_im.reshape(bn, -1, SSM_STATE))
            else:
                h_re = past[3][i].reshape(t, -1)
                h_im = past[4][i].reshape(t, -1)
                g, s_re, s_im = _s5_sample(u[0], h_re, h_im, p['s5_bbar_re'][i], p['s5_bbar_im'][i],
                                           p['s5_c_re'][i], p['s5_c_im'][i],
                                           p['s5_pow_re'][i][0:1], p['s5_pow_im'][i][0:1], p['ssm_d'][i])
                g = g[None]
                ssm_re.append(s_re.reshape(t, -1, SSM_STATE))
                ssm_im.append(s_im.reshape(t, -1, SSM_STATE))
            x, h = _proj_residual(x, g, (p['ssm_w_glu_out'], p['ssm_w_glu_gate']), g1, i,
                                  nxt=(p['norm_ffn'][layer], sh2, sc2, act_dtype))
            x, h_mix = _swiglu_residual(x, h, p['ffn_w_gate'][:, None], p['ffn_w_up'][:, None],
                                        p['ffn_w_down'][:, None], g2, i, nxt=nxt)
        else:
            h = h_mix
            if past is None:
                k_t, v_t, k16, qt16, vt16, pen = _qkv_prompt(h, p['attn_w_qkv'], i)
                a = _moba_prompt(qt16, k16, vt16, pen)
                new_k.append(k_t.reshape(bn, N_HEADS, HEAD_DIM, t))
                new_v.append(v_t.reshape(bn, N_HEADS, HEAD_DIM, t))
            else:
                qkv = _matmul(h[0], p['attn_w_qkv'], i)
                q, k, v = (qkv[:, j * d:(j + 1) * d].reshape(t, N_HEADS, HEAD_DIM, 1) for j in range(3))
                a = _moba_sample(q, k, v, past[0], past[1], past[2], i).reshape(1, t, d)
                new_k.append(k.reshape(t, N_HEADS, HEAD_DIM, 1))
                new_v.append(v.reshape(t, N_HEADS, HEAD_DIM, 1))
            x, h, comb = _proj_residual(x, a, (p['attn_w_o'],), g1, i,
                                        nxt=(p['norm_ffn'][layer], sh2, sc2, act_dtype,
                                             p['moe_w_router'][i], p['moe_b_router'][i]))
            x, h_mix = _swiglu_residual(x, h, p['moe_w_gate'], p['moe_w_up'], p['moe_w_down'], g2, i,
                                        combine=comb, nxt=nxt)
    y = h_mix
    new_k, new_v = (jnp.stack(z).transpose(0, 1, 4, 2, 3) for z in (new_k, new_v))
    return y, jnp.stack(ssm_re), jnp.stack(ssm_im), new_k, new_v


def kernel(x_prompt, x_sample, cache_k, cache_v, page_table, state_ssm_re, state_ssm_im,
           c_prompt, c_sample, ada_w, ada_b, norm_mix, norm_ffn, norm_final,
           ssm_a_re, ssm_a_im, ssm_log_dt, ssm_b_re, ssm_b_im, ssm_c_re, ssm_c_im, ssm_d,
           ssm_w_glu_out, ssm_w_glu_gate, attn_w_qkv, attn_w_o,
           ffn_w_gate, ffn_w_up, ffn_w_down,
           moe_w_router, moe_b_router, moe_w_gate, moe_w_up, moe_w_down):
    bn, t, d = x_prompt.shape
    dec = x_sample.shape[0]
    depth = ada_w.shape[0]
    assert x_sample.shape[1] == 1 and d == N_HEADS * HEAD_DIM and t % MOBA_BLOCK == 0

    n_cond = bn + dec
    c_all = jnp.pad(jnp.concatenate([c_prompt, c_sample], axis=0), ((0, -n_cond % SUBLANES), (0, 0)))
    mod = _adaln(c_all, ada_w, ada_b)
    mods_p = [[mod[l, :bn, j * d:(j + 1) * d].reshape(bn, 1, d) for j in range(6)] for l in range(depth)]
    mods_s = [[mod[l, bn:n_cond, j * d:(j + 1) * d].reshape(1, dec, d) for j in range(6)] for l in range(depth)]

    pow_re, pow_im, bbar_re, bbar_im = _s5_prep(ssm_a_re, ssm_a_im, ssm_log_dt, ssm_b_re, ssm_b_im,
                                                S5_TILE // SUBLANES)
    c_re_bd, c_im_bd = _blockdiag_c(ssm_c_re), _blockdiag_c(ssm_c_im)
    shared = {
        'norm_mix': norm_mix, 'norm_ffn': norm_ffn, 'norm_final': norm_final, 'ssm_d': ssm_d,
        's5_pow_re': pow_re, 's5_pow_im': pow_im, 's5_bbar_re': bbar_re, 's5_bbar_im': bbar_im,
        's5_bb16': jnp.concatenate([bbar_re, bbar_im], axis=-1).astype(BF16),
        's5_c_re': c_re_bd, 's5_c_im': c_im_bd,
        's5_cre16': c_re_bd.astype(BF16), 's5_cim16': c_im_bd.astype(BF16),
        'moe_w_router': moe_w_router, 'moe_b_router': moe_b_router,
    }
    big = {
        'ssm_w_glu_out': ssm_w_glu_out, 'ssm_w_glu_gate': ssm_w_glu_gate,
        'attn_w_qkv': attn_w_qkv, 'attn_w_o': attn_w_o,
        'ffn_w_gate': ffn_w_gate, 'ffn_w_up': ffn_w_up, 'ffn_w_down': ffn_w_down,
        'moe_w_gate': moe_w_gate, 'moe_w_up': moe_w_up, 'moe_w_down': moe_w_down,
    }
    p_prompt = dict(shared, **{name: w.astype(BF16) for name, w in big.items()})
    p_sample = dict(shared, **big)

    y_p, re_p, im_p, k_p, v_p = _trunk(x_prompt, mods_p, p_prompt, None)

    past = (cache_k.transpose(0, 1, 3, 4, 2), cache_v.transpose(0, 1, 3, 4, 2),
            page_table, state_ssm_re, state_ssm_im)
    y_s, re_s, im_s, k_s, v_s = _trunk(x_sample.reshape(1, dec, d), mods_s, p_sample, past)
    return (y_p, y_s.reshape(dec, 1, d), re_p, im_p, re_s, im_s, k_p, v_p, k_s, v_s)
```

```python
import functools

import jax
import jax.numpy as jnp
from jax import lax
from jax.experimental import pallas as pl
from jax.experimental.pallas import tpu as pltpu

F32 = jnp.float32
BF16 = jnp.bfloat16
HI = lax.Precision.HIGHEST
NEG_INF = float("-inf")
LOG2E = 1.4426950408889634

SUBLANES = 8
LANES = 128

RMS_EPS = 1e-6
SSM_GROUP = 16
SSM_STATE = 64
PACK_GROUPS = LANES // SSM_GROUP
PACK_STATES = PACK_GROUPS * SSM_STATE
N_HEADS = 16
HEAD_DIM = 64
MOBA_BLOCK = 256
MOBA_TOPK = 3
TOP_K = 2

ROW_TILE = 512
S5_TILE = 512
MOBA_HEADS_PER_STEP = 8
SAMPLE_BLOCKS_PER_STEP = 8
VMEM_LIMIT = 56 * 1024 * 1024


def _cparams(sem):
    return pltpu.CompilerParams(dimension_semantics=sem, vmem_limit_bytes=VMEM_LIMIT)


def _wdot(a, w):
    if w.dtype == F32:
        return jnp.dot(a.astype(F32), w, precision=HI, preferred_element_type=F32)
    return jnp.dot(a.astype(BF16), w, preferred_element_type=F32)


def _row_tile(t):
    return ROW_TILE if t % ROW_TILE == 0 else t


def _mod_spec(arr, tm):
    d = arr.shape[-1]
    if arr.shape[1] == 1:
        return pl.BlockSpec((1, 1, d), lambda b, t: (b, 0, 0))
    return pl.BlockSpec((1, tm, d), lambda b, t: (b, t, 0))


def _adaln_kernel(c_ref, w_ref, b_ref, o_ref):
    c = c_ref[...]
    s = c * jax.nn.sigmoid(c)
    o_ref[0] = jnp.dot(s, w_ref[0], precision=HI, preferred_element_type=F32) + b_ref[0]


def _adaln(c_all, ada_w, ada_b):
    depth, d, n = ada_w.shape
    r = c_all.shape[0]
    tn = 1536
    return pl.pallas_call(
        _adaln_kernel,
        grid=(depth, n // tn),
        in_specs=[pl.BlockSpec((r, d), lambda l, j: (0, 0)),
                  pl.BlockSpec((1, d, tn), lambda l, j: (l, 0, j)),
                  pl.BlockSpec((1, 1, tn), lambda l, j: (l, 0, j))],
        out_specs=pl.BlockSpec((1, r, tn), lambda l, j: (l, 0, j)),
        out_shape=jax.ShapeDtypeStruct((depth, r, n), F32),
        compiler_params=_cparams(("arbitrary", "arbitrary")),
        name="adaln",
    )(c_all, ada_w, ada_b.reshape(depth, 1, n))


def _rms(x, g):
    return x * lax.rsqrt(jnp.mean(x * x, axis=-1, keepdims=True) + RMS_EPS) * g


def _modulate_kernel(x_ref, g_ref, sh_ref, sc_ref, o_ref):
    h = _rms(x_ref[0], g_ref[...]) * (1.0 + sc_ref[0]) + sh_ref[0]
    o_ref[0] = h.astype(o_ref.dtype)


def _modulate_router_kernel(x_ref, g_ref, sh_ref, sc_ref, wr_ref, br_ref, o_ref, comb_ref):
    h = _rms(x_ref[0], g_ref[...]) * (1.0 + sc_ref[0]) + sh_ref[0]
    o_ref[0] = h.astype(o_ref.dtype)
    logits = jnp.dot(h, wr_ref[...], precision=HI, preferred_element_type=F32) + br_ref[...]
    lane = lax.broadcasted_iota(jnp.int32, logits.shape, 1)
    m1 = jnp.max(logits, axis=-1, keepdims=True)
    i1 = jnp.min(jnp.where(logits == m1, lane, LANES), axis=-1, keepdims=True)
    rest = jnp.where(lane == i1, NEG_INF, logits)
    m2 = jnp.max(rest, axis=-1, keepdims=True)
    i2 = jnp.min(jnp.where(rest == m2, lane, LANES), axis=-1, keepdims=True)
    e2 = jnp.exp(m2 - m1)
    den = 1.0 + e2
    comb_ref[0] = jnp.where(lane == i1, 1.0 / den, 0.0) + jnp.where(lane == i2, e2 / den, 0.0)


def _modulate(x, g, shift, scale, out_dtype, router=None):
    bn, t, d = x.shape
    tm = _row_tile(t)
    grid = (bn, t // tm)
    x_spec = pl.BlockSpec((1, tm, d), lambda b, i: (b, i, 0))
    in_specs = [x_spec, pl.BlockSpec((1, d), lambda b, i: (0, 0)), _mod_spec(shift, tm), _mod_spec(scale, tm)]
    args = [x, g.reshape(1, d), shift, scale]
    if router is None:
        return pl.pallas_call(
            _modulate_kernel, grid=grid, in_specs=in_specs, out_specs=x_spec,
            out_shape=jax.ShapeDtypeStruct((bn, t, d), out_dtype),
            compiler_params=_cparams(("arbitrary", "arbitrary")), name="modulate",
        )(*args)
    w_router, b_router = router
    n_exp = w_router.shape[1]
    wr = jnp.pad(w_router, ((0, 0), (0, LANES - n_exp)))
    br = jnp.pad(b_router.reshape(1, n_exp), ((0, 0), (0, LANES - n_exp)), constant_values=NEG_INF)
    in_specs += [pl.BlockSpec((d, LANES), lambda b, i: (0, 0)), pl.BlockSpec((1, LANES), lambda b, i: (0, 0))]
    return pl.pallas_call(
        _modulate_router_kernel, grid=grid, in_specs=in_specs,
        out_specs=[x_spec, pl.BlockSpec((1, tm, LANES), lambda b, i: (b, i, 0))],
        out_shape=[jax.ShapeDtypeStruct((bn, t, d), out_dtype), jax.ShapeDtypeStruct((bn, t, LANES), F32)],
        compiler_params=_cparams(("arbitrary", "arbitrary")), name="modulate_router",
    )(*args, wr, br)


def _store_residual(x_new, out_refs):
    if len(out_refs) == 1:
        out_refs[0][0] = x_new
        return
    g_ref, sh_ref, sc_ref, o_ref, h_ref = out_refs
    o_ref[0] = x_new
    h_ref[0] = (_rms(x_new, g_ref[...]) * (1.0 + sc_ref[0]) + sh_ref[0]).astype(h_ref.dtype)


def _proj_residual_kernel(x_ref, a_ref, w_ref, gate_ref, *out_refs):
    y = _wdot(a_ref[0], w_ref[0])
    _store_residual(x_ref[0] + gate_ref[0] * y, out_refs)


def _glu_residual_kernel(x_ref, a_ref, wo_ref, wg_ref, gate_ref, *out_refs):
    a = a_ref[0]
    y = _wdot(a, wo_ref[0])
    z = _wdot(a, wg_ref[0])
    _store_residual(x_ref[0] + gate_ref[0] * (y * jax.nn.sigmoid(z)), out_refs)


def _next_modulate_io(nxt, d, tm, x_spec, shape, const_map, mod_spec):
    if nxt is None:
        return [], [], x_spec, jax.ShapeDtypeStruct(shape, F32)
    g, shift, scale, dtype = nxt
    args = [g.reshape(1, d), shift, scale]
    specs = [pl.BlockSpec((1, d), const_map), mod_spec(shift), mod_spec(scale)]
    return args, specs, [x_spec, x_spec], [jax.ShapeDtypeStruct(shape, F32), jax.ShapeDtypeStruct(shape, dtype)]


def _proj_residual(x, a, weights, gate, layer, nxt=None):
    bn, t, d = x.shape
    tm = _row_tile(t)
    x_spec = pl.BlockSpec((1, tm, d), lambda b, i: (b, i, 0))
    w_spec = pl.BlockSpec((1, d, d), lambda b, i: (layer, 0, 0))
    kern = _proj_residual_kernel if len(weights) == 1 else _glu_residual_kernel
    n_args, n_specs, out_specs, out_shape = _next_modulate_io(
        nxt, d, tm, x_spec, (bn, t, d), lambda b, i: (0, 0), lambda m: _mod_spec(m, tm))
    return pl.pallas_call(
        kern, grid=(bn, t // tm),
        in_specs=[x_spec, x_spec] + [w_spec] * len(weights) + [_mod_spec(gate, tm)] + n_specs,
        out_specs=out_specs, out_shape=out_shape,
        compiler_params=_cparams(("arbitrary", "arbitrary")), name="proj_residual",
    )(x, a, *weights, gate, *n_args)


def _swiglu_kernel(x_ref, h_ref, wg_ref, wu_ref, wd_ref, gate_ref, *rest):
    out_refs, acc_ref = rest[:-1], rest[-1]
    e, f = pl.program_id(2), pl.program_id(3)

    @pl.when((e == 0) & (f == 0))
    def _():
        acc_ref[...] = jnp.zeros_like(acc_ref)

    h = h_ref[0]
    a = _wdot(h, wg_ref[0, 0])
    u = _wdot(h, wu_ref[0, 0])
    acc_ref[...] += _wdot(a * jax.nn.sigmoid(a) * u, wd_ref[0, 0])

    @pl.when((e == pl.num_programs(2) - 1) & (f == pl.num_programs(3) - 1))
    def _():
        _store_residual(x_ref[0] + gate_ref[0] * acc_ref[...], out_refs)


def _moe_swiglu_kernel(x_ref, h_ref, comb_ref, wg_ref, wu_ref, wd_ref, gate_ref, *rest):
    out_refs, acc_ref = rest[:-1], rest[-1]
    e, f = pl.program_id(2), pl.program_id(3)

    @pl.when((e == 0) & (f == 0))
    def _():
        acc_ref[...] = jnp.zeros_like(acc_ref)

    h = h_ref[0]
    a = _wdot(h, wg_ref[0, 0])
    u = _wdot(h, wu_ref[0, 0])
    act = a * jax.nn.sigmoid(a) * u
    comb = comb_ref[0]
    lane = lax.broadcasted_iota(jnp.int32, comb.shape, 1)
    w_e = jnp.sum(jnp.where(lane == e, comb, 0.0), axis=-1, keepdims=True)
    acc_ref[...] += w_e * _wdot(act, wd_ref[0, 0])

    @pl.when((e == pl.num_programs(2) - 1) & (f == pl.num_programs(3) - 1))
    def _():
        _store_residual(x_ref[0] + gate_ref[0] * acc_ref[...], out_refs)


def _swiglu_residual(x, h, w_gate, w_up, w_down, gate, layer, combine=None, nxt=None):
    bn, t, d = x.shape
    _, n_exp, _, ff = w_gate.shape
    tm = _row_tile(t)
    tf = 1408
    assert ff % tf == 0
    x_spec = pl.BlockSpec((1, tm, d), lambda b, i, e, f: (b, i, 0))
    wi_spec = pl.BlockSpec((1, 1, d, tf), lambda b, i, e, f: (layer, e, 0, f))
    wd_spec = pl.BlockSpec((1, 1, tf, d), lambda b, i, e, f: (layer, e, f, 0))
    def mod_spec(arr):
        if arr.shape[1] == 1:
            return pl.BlockSpec((1, 1, d), lambda b, i, e, f: (b, 0, 0))
        return pl.BlockSpec((1, tm, d), lambda b, i, e, f: (b, i, 0))

    n_args, n_specs, out_specs, out_shape = _next_modulate_io(
        nxt, d, tm, x_spec, (bn, t, d), lambda b, i, e, f: (0, 0), mod_spec)
    if combine is None:
        kern, in_specs, args = _swiglu_kernel, [x_spec, x_spec], [x, h]
    else:
        kern = _moe_swiglu_kernel
        in_specs = [x_spec, x_spec, pl.BlockSpec((1, tm, LANES), lambda b, i, e, f: (b, i, 0))]
        args = [x, h, combine]
    return pl.pallas_call(
        kern, grid=(bn, t // tm, n_exp, ff // tf),
        in_specs=in_specs + [wi_spec, wi_spec, wd_spec, mod_spec(gate)] + n_specs,
        out_specs=out_specs, out_shape=out_shape,
        scratch_shapes=[pltpu.VMEM((tm, d), F32)],
        compiler_params=_cparams(("arbitrary",) * 4), name="swiglu",
    )(*args, w_gate, w_up, w_down, gate, *n_args)


def _s5_prep_kernel(are_ref, aim_ref, ldt_ref, bre_ref, bim_ref,
                    pre_ref, pim_ref, obr_ref, obi_ref, *, seg):
    a_re, a_im = are_ref[0], aim_ref[0]
    dt = jnp.exp(ldt_ref[0])
    steps = lax.broadcasted_iota(jnp.int32, (seg, PACK_STATES), 0).astype(F32) + 1.0
    mag = jnp.exp(steps * (dt * a_re))
    ang = steps * (dt * a_im)
    pre_ref[0] = mag * jnp.cos(ang)
    pim_ref[0] = mag * jnp.sin(ang)
    mag1 = jnp.exp(dt * a_re)
    ab_re = mag1 * jnp.cos(dt * a_im)
    ab_im = mag1 * jnp.sin(dt * a_im)
    den = a_re * a_re + a_im * a_im
    num_re = ab_re - 1.0
    coef_re = (num_re * a_re + ab_im * a_im) / den
    coef_im = (ab_im * a_re - num_re * a_im) / den
    b_re, b_im = bre_ref[0, 0], bim_ref[0, 0]
    obr_ref[0, 0] = coef_re * b_re - coef_im * b_im
    obi_ref[0, 0] = coef_re * b_im + coef_im * b_re


def _blockdiag_b(b):
    nl, g, p, j = b.shape
    b5 = b.reshape(nl, g // PACK_GROUPS, PACK_GROUPS, p, j).transpose(0, 1, 2, 4, 3)
    eye = jnp.eye(PACK_GROUPS, dtype=b.dtype)
    out = b5[:, :, :, :, None, :] * eye[None, None, :, None, :, None]
    return out.reshape(nl, g // PACK_GROUPS, PACK_GROUPS * j, PACK_GROUPS * p)


def _blockdiag_c(c):
    nl, g, j, p = c.shape
    c5 = c.reshape(nl, g // PACK_GROUPS, PACK_GROUPS, j, p).transpose(0, 1, 2, 4, 3)
    eye = jnp.eye(PACK_GROUPS, dtype=c.dtype)
    out = c5[:, :, :, :, None, :] * eye[None, None, :, None, :, None]
    return out.reshape(nl, g // PACK_GROUPS, PACK_GROUPS * p, PACK_GROUPS * j)


def _s5_prep(a_re, a_im, log_dt, b_re, b_im, seg):
    nl, g, p = a_re.shape
    n_state = g * p
    n_pack = g // PACK_GROUPS
    row = lambda a: a.reshape(nl, 1, n_state)
    ldt = jnp.broadcast_to(log_dt[:, :, None], (nl, g, p))
    braw_re, braw_im = _blockdiag_b(b_re), _blockdiag_b(b_im)
    vec_spec = pl.BlockSpec((1, 1, PACK_STATES), lambda l, k: (l, 0, k))
    mat_spec = pl.BlockSpec((1, 1, LANES, PACK_STATES), lambda l, k: (l, k, 0, 0))
    pow_spec = pl.BlockSpec((1, seg, PACK_STATES), lambda l, k: (l, 0, k))
    return pl.pallas_call(
        functools.partial(_s5_prep_kernel, seg=seg),
        grid=(nl, n_pack),
        in_specs=[vec_spec, vec_spec, vec_spec, mat_spec, mat_spec],
        out_specs=[pow_spec, pow_spec, mat_spec, mat_spec],
        out_shape=[jax.ShapeDtypeStruct((nl, seg, n_state), F32)] * 2
        + [jax.ShapeDtypeStruct(braw_re.shape, F32)] * 2,
        compiler_params=_cparams(("arbitrary", "arbitrary")), name="s5_prep",
    )(row(a_re), row(a_im), row(ldt), braw_re, braw_im)


def _s5_prompt_kernel(u_ref, bb_ref, cre_ref, cim_ref, pre_ref, pim_ref, d_ref,
                      g_ref, fre_ref, fim_ref,
                      uperm, sre, sim, gperm, car_re, car_im, ini_re, ini_im, *, seg, chunk):
    n_pack = bb_ref.shape[0]
    n_state = sre.shape[1]

    @pl.when(pl.program_id(1) == 0)
    def _():
        car_re[...] = jnp.zeros_like(car_re)
        car_im[...] = jnp.zeros_like(car_im)

    for pk in range(n_pack):
        for k in range(SUBLANES):
            uperm[pk, pl.ds(k, seg, stride=SUBLANES), :] = (
                u_ref[0, k * seg:(k + 1) * seg, pk * LANES:(pk + 1) * LANES])

    for pk in range(n_pack):
        z = jnp.dot(uperm[pk].astype(BF16), bb_ref[pk], preferred_element_type=F32)
        sre[:, pk * PACK_STATES:(pk + 1) * PACK_STATES] = z[:, :PACK_STATES]
        sim[:, pk * PACK_STATES:(pk + 1) * PACK_STATES] = z[:, PACK_STATES:]

    for c in range(n_state // chunk):
        cs = slice(c * chunk, (c + 1) * chunk)
        lam_re = jnp.broadcast_to(pre_ref[0:1, cs], (SUBLANES, chunk))
        lam_im = jnp.broadcast_to(pim_ref[0:1, cs], (SUBLANES, chunk))

        def scan_step(n, carry, cs=cs, lam_re=lam_re, lam_im=lam_im):
            h_re, h_im = carry
            rows = pl.ds(pl.multiple_of(n * SUBLANES, SUBLANES), SUBLANES)
            n_re = lam_re * h_re - lam_im * h_im + sre[rows, cs]
            n_im = lam_re * h_im + lam_im * h_re + sim[rows, cs]
            sre[rows, cs] = n_re
            sim[rows, cs] = n_im
            return n_re, n_im

        zero = jnp.zeros((SUBLANES, chunk), F32)
        end_re, end_im = lax.fori_loop(0, seg, scan_step, (zero, zero), unroll=4)

        seg_re, seg_im = pre_ref[seg - 1:seg, cs], pim_ref[seg - 1:seg, cs]
        cur_re, cur_im = car_re[:, cs], car_im[:, cs]
        for k in range(SUBLANES):
            ini_re[k:k + 1, cs] = cur_re
            ini_im[k:k + 1, cs] = cur_im
            nxt_re = end_re[k:k + 1] + seg_re * cur_re - seg_im * cur_im
            nxt_im = end_im[k:k + 1] + seg_re * cur_im + seg_im * cur_re
            cur_re, cur_im = nxt_re, nxt_im
        car_re[:, cs] = cur_re
        car_im[:, cs] = cur_im

        in_re, in_im = ini_re[:, cs], ini_im[:, cs]

        def fix_step(n, carry, cs=cs, in_re=in_re, in_im=in_im):
            rows = pl.ds(pl.multiple_of(n * SUBLANES, SUBLANES), SUBLANES)
            p_re, p_im = pre_ref[pl.ds(n, 1), cs], pim_ref[pl.ds(n, 1), cs]
            sre[rows, cs] = sre[rows, cs] + p_re * in_re - p_im * in_im
            sim[rows, cs] = sim[rows, cs] + p_re * in_im + p_im * in_re
            return carry

        lax.fori_loop(0, seg, fix_step, 0, unroll=4)

    fre_ref[0] = car_re[...]
    fim_ref[0] = car_im[...]

    for pk in range(n_pack):
        ss = slice(pk * PACK_STATES, (pk + 1) * PACK_STATES)
        ls = slice(pk * LANES, (pk + 1) * LANES)
        y = (jnp.dot(sre[:, ss].astype(BF16), cre_ref[pk], preferred_element_type=F32)
             - jnp.dot(sim[:, ss].astype(BF16), cim_ref[pk], preferred_element_type=F32))
        y = y + d_ref[:, ls] * uperm[pk]
        gperm[pk] = jax.nn.gelu(y, approximate=True)

    for pk in range(n_pack):
        for k in range(SUBLANES):
            g_ref[0, k * seg:(k + 1) * seg, pk * LANES:(pk + 1) * LANES] = (
                gperm[pk, pl.ds(k, seg, stride=SUBLANES), :].astype(g_ref.dtype))


def _s5_prompt(u, bb16, cre16, cim16, pow_re, pow_im, d_skip):
    bn, t, d = u.shape
    n_pack = bb16.shape[0]
    n_state = pow_re.shape[1]
    tt = S5_TILE
    seg = tt // SUBLANES
    assert t % tt == 0 and pow_re.shape[0] == seg
    const = lambda shape: pl.BlockSpec(shape, lambda b, i: (0,) * len(shape))
    u_spec = pl.BlockSpec((1, tt, d), lambda b, i: (b, i, 0))
    fin_spec = pl.BlockSpec((1, 1, n_state), lambda b, i: (b, 0, 0))
    return pl.pallas_call(
        functools.partial(_s5_prompt_kernel, seg=seg, chunk=512),
        grid=(bn, t // tt),
        in_specs=[u_spec, const(bb16.shape), const(cre16.shape), const(cim16.shape),
                  const(pow_re.shape), const(pow_im.shape), const((1, d))],
        out_specs=[u_spec, fin_spec, fin_spec],
        out_shape=[jax.ShapeDtypeStruct((bn, t, d), BF16),
                   jax.ShapeDtypeStruct((bn, 1, n_state), F32),
                   jax.ShapeDtypeStruct((bn, 1, n_state), F32)],
        scratch_shapes=[pltpu.VMEM((n_pack, tt, LANES), F32), pltpu.VMEM((tt, n_state), F32),
                        pltpu.VMEM((tt, n_state), F32), pltpu.VMEM((n_pack, tt, LANES), F32),
                        pltpu.VMEM((1, n_state), F32), pltpu.VMEM((1, n_state), F32),
                        pltpu.VMEM((SUBLANES, n_state), F32), pltpu.VMEM((SUBLANES, n_state), F32)],
        compiler_params=_cparams(("arbitrary", "arbitrary")), name="s5_prompt",
    )(u, bb16, cre16, cim16, pow_re, pow_im, d_skip.reshape(1, d))


def _s5_sample_kernel(u_ref, hre_ref, him_ref, bre_ref, bim_ref, cre_ref, cim_ref,
                      lre_ref, lim_ref, d_ref, g_ref, sre_ref, sim_ref):
    for pk in range(bre_ref.shape[0]):
        ss = slice(pk * PACK_STATES, (pk + 1) * PACK_STATES)
        ls = slice(pk * LANES, (pk + 1) * LANES)
        u = u_ref[:, ls]
        lam_re, lam_im = lre_ref[:, ss], lim_ref[:, ss]
        h_re, h_im = hre_ref[:, ss], him_ref[:, ss]
        s_re = (jnp.dot(u, bre_ref[pk], precision=HI, preferred_element_type=F32)
                + (lam_re * h_re - lam_im * h_im))
        s_im = (jnp.dot(u, bim_ref[pk], precision=HI, preferred_element_type=F32)
                + (lam_re * h_im + lam_im * h_re))
        sre_ref[:, ss] = s_re
        sim_ref[:, ss] = s_im
        y = (jnp.dot(s_re, cre_ref[pk], precision=HI, preferred_element_type=F32)
             - jnp.dot(s_im, cim_ref[pk], precision=HI, preferred_element_type=F32))
        y = y + d_ref[:, ls] * u
        g_ref[:, ls] = jax.nn.gelu(y, approximate=True).astype(g_ref.dtype)


def _s5_sample(u, h_re, h_im, bbar_re, bbar_im, c_re, c_im, lam_re, lam_im, d_skip):
    rows, d = u.shape
    n_state = h_re.shape[1]
    return pl.pallas_call(
        _s5_sample_kernel,
        out_shape=[jax.ShapeDtypeStruct((rows, d), F32),
                   jax.ShapeDtypeStruct((rows, n_state), F32),
                   jax.ShapeDtypeStruct((rows, n_state), F32)],
        compiler_params=pltpu.CompilerParams(vmem_limit_bytes=VMEM_LIMIT), name="s5_sample",
    )(u, h_re, h_im, bbar_re, bbar_im, c_re, c_im, lam_re, lam_im, d_skip.reshape(1, d))


def _top_blocks_penalty(gate, blk, n_valid_below):
    gate = jnp.where(blk < n_valid_below, gate, NEG_INF)
    n_blk = gate.shape[0]
    sel = jnp.zeros(gate.shape, F32)
    for _ in range(MOBA_TOPK):
        m = jnp.max(gate, axis=0, keepdims=True)
        idx = jnp.min(jnp.where(gate == m, blk, n_blk), axis=0, keepdims=True)
        hit = blk == idx
        sel = jnp.where(hit & (m > NEG_INF), 1.0, sel)
        gate = jnp.where(hit, NEG_INF, gate)
    return jnp.where(sel > 0.0, 0.0, NEG_INF)


def _qkv_prompt_kernel(h_ref, w_ref, kt_ref, vt_ref, k16_ref, qt16_ref, vt16_ref, pen_ref, kmean):
    i = pl.program_id(1)
    d = k16_ref.shape[-1]
    n_blk = kmean.shape[0]

    @pl.when(i == 0)
    def _():
        kmean[...] = jnp.zeros_like(kmean)

    r = jnp.dot(h_ref[0], w_ref[0], preferred_element_type=F32)
    q, k, v = r[:, :d], r[:, d:2 * d], r[:, 2 * d:]
    k16_ref[0] = k.astype(BF16)
    q_t, v_t = q.T, v.T
    kt_ref[0] = k.T
    vt_ref[0] = v_t
    qt16_ref[0] = (q_t * (LOG2E * HEAD_DIM ** -0.5)).astype(BF16)
    vt16_ref[0, 0] = v_t.astype(BF16)

    km = kmean[...]
    blk = lax.broadcasted_iota(jnp.int32, (n_blk, MOBA_BLOCK), 0)
    for h in range(N_HEADS):
        hs = slice(h * HEAD_DIM, (h + 1) * HEAD_DIM)
        gate = jnp.dot(km[:, hs], q_t[hs, :], precision=HI, preferred_element_type=F32)
        pen_ref[0, h] = _top_blocks_penalty(gate, blk, i)

    row = lax.broadcasted_iota(jnp.int32, km.shape, 0)
    kmean[...] = jnp.where(row == i, jnp.mean(k, axis=0, keepdims=True), km)


def _qkv_prompt(h16, w16, layer):
    bn, t, d = h16.shape
    n_blk = t // MOBA_BLOCK
    tile = pl.BlockSpec((1, MOBA_BLOCK, d), lambda b, i: (b, i, 0))
    tile_t = pl.BlockSpec((1, d, MOBA_BLOCK), lambda b, i: (b, 0, i))
    return pl.pallas_call(
        _qkv_prompt_kernel, grid=(bn, n_blk),
        in_specs=[tile, pl.BlockSpec((1,) + w16.shape[1:], lambda b, i: (layer, 0, 0))],
        out_specs=[tile_t, tile_t, tile, tile_t,
                   pl.BlockSpec((1, 1, d, MOBA_BLOCK), lambda b, i: (b, i, 0, 0)),
                   pl.BlockSpec((1, N_HEADS, n_blk, MOBA_BLOCK), lambda b, i: (b, 0, 0, i))],
        out_shape=[jax.ShapeDtypeStruct((bn, d, t), F32), jax.ShapeDtypeStruct((bn, d, t), F32),
                   jax.ShapeDtypeStruct((bn, t, d), BF16),
                   jax.ShapeDtypeStruct((bn, d, t), BF16),
                   jax.ShapeDtypeStruct((bn, n_blk, d, MOBA_BLOCK), BF16),
                   jax.ShapeDtypeStruct((bn, N_HEADS, n_blk, t), F32)],
        scratch_shapes=[pltpu.VMEM((n_blk, d), F32)],
        compiler_params=_cparams(("arbitrary", "arbitrary")), name="qkv_prompt",
    )(h16, w16)


def _moba_prompt_kernel(slope_ref, qt_ref, k_ref, vt_ref, pen_ref, o_ref, s_ring, *, n_heads):
    hp, i = pl.program_id(1), pl.program_id(2)
    blk = MOBA_BLOCK
    k_iota = lax.broadcasted_iota(jnp.int32, (blk, blk), 0)
    q_iota = lax.broadcasted_iota(jnp.int32, (blk, blk), 1)
    causal = k_iota <= q_iota
    k_local = k_iota.astype(F32)
    head_row = lax.broadcasted_iota(jnp.int32, (LANES, blk), 0) // HEAD_DIM
    heads = range(n_heads)
    slab = [slice((a // 2) * LANES, (a // 2 + 1) * LANES) for a in heads]
    q_h = []
    for a in heads:
        q_slab = qt_ref[0, slab[a], :]
        q_h.append(jnp.where(head_row == a % 2, q_slab, jnp.zeros_like(q_slab)))
    slope = [slope_ref[n_heads * hp + a] for a in heads]
    k_bias = [slope[a] * k_local for a in heads]
    rows = [slice(a * HEAD_DIM, (a + 1) * HEAD_DIM) for a in heads]

    def scores(a, n):
        k_n = k_ref[0, pl.ds(pl.multiple_of(n * blk, blk), blk), slab[a]]
        return jnp.dot(k_n, q_h[a], preferred_element_type=F32) + k_bias[a]

    def offset(a, n):
        shift = slope[a] * jnp.full((1, blk), (n - i) * blk, jnp.int32).astype(F32)
        return shift + pen_ref[0, a, pl.ds(n, 1), :]

    ones_rows = jnp.ones((2 * SUBLANES, blk), BF16)

    def weighted_values(a, n, p):
        v_aug = jnp.concatenate([vt_ref[0, n, rows[a], :], ones_rows], axis=0)
        return jnp.dot(v_aug, p.astype(BF16), preferred_element_type=F32)

    init = []
    for a in heads:
        s = jnp.where(causal, scores(a, i), NEG_INF)
        m0 = jnp.max(s, axis=0, keepdims=True)
        init += [m0, weighted_values(a, i, jnp.exp2(s - m0))]

    def update(carry, blocks, get_scores):
        out = []
        for a in heads:
            m, acc = carry[2 * a:2 * a + 2]
            off_blk = [offset(a, n) for n in blocks]
            m_new = m
            for k, off in enumerate(off_blk):
                m_new = jnp.maximum(m_new, jnp.max(get_scores(a, k), axis=0, keepdims=True) + off)
            acc = jnp.exp2(m - m_new) * acc
            for k, (n, off) in enumerate(zip(blocks, off_blk)):
                acc = acc + weighted_values(a, n, jnp.exp2(get_scores(a, k) - (m_new - off)))
            out += [m_new, acc]
        return tuple(out)

    n_pairs = i // 2

    def store_pair_scores(slot, j):
        for k, n in enumerate((jnp.minimum(2 * j, i), jnp.minimum(2 * j + 1, i))):
            for a in heads:
                s_ring[slot, 2 * a + k] = scores(a, n)

    def pair_step(j, state):
        slot = j & 1
        state = update(state, (2 * j, 2 * j + 1), lambda a, k: s_ring[slot, 2 * a + k])
        store_pair_scores(1 - slot, j + 1)
        return state

    store_pair_scores(0, 0)
    carry = lax.fori_loop(0, n_pairs, pair_step, tuple(init))

    def single_step(n, state):
        tiles = [scores(a, n) for a in heads]
        return update(state, (n,), lambda a, k: tiles[a])

    carry = lax.fori_loop(2 * n_pairs, i, single_step, carry)
    outs = [carry[2 * a + 1][:HEAD_DIM] / carry[2 * a + 1][HEAD_DIM:HEAD_DIM + 1] for a in heads]
    o_ref[0] = jnp.concatenate(outs, axis=0).T.astype(o_ref.dtype)


def _alibi_slopes():
    return 2.0 ** (-8.0 * jnp.arange(1, N_HEADS + 1, dtype=F32) / N_HEADS)


def _moba_prompt(qt16, k16, vt16, pen):
    bn, d, t = qt16.shape
    n_blk = t // MOBA_BLOCK
    hps = MOBA_HEADS_PER_STEP
    width = hps * HEAD_DIM
    return pl.pallas_call(
        functools.partial(_moba_prompt_kernel, n_heads=hps),
        grid_spec=pltpu.PrefetchScalarGridSpec(
            num_scalar_prefetch=0,
            grid=(bn, N_HEADS // hps, n_blk),
            in_specs=[pl.BlockSpec(memory_space=pltpu.SMEM),
                      pl.BlockSpec((1, width, MOBA_BLOCK), lambda b, hp, i: (b, hp, i)),
                      pl.BlockSpec((1, t, width), lambda b, hp, i: (b, 0, hp)),
                      pl.BlockSpec((1, n_blk, width, MOBA_BLOCK), lambda b, hp, i: (b, 0, hp, 0)),
                      pl.BlockSpec((1, hps, n_blk, MOBA_BLOCK), lambda b, hp, i: (b, hp, 0, i))],
            out_specs=pl.BlockSpec((1, MOBA_BLOCK, width), lambda b, hp, i: (b, i, hp)),
            scratch_shapes=[pltpu.VMEM((2, 2 * hps, MOBA_BLOCK, MOBA_BLOCK), F32)],
        ),
        out_shape=jax.ShapeDtypeStruct((bn, t, d), BF16),
        compiler_params=_cparams(("arbitrary",) * 3), name="moba_prompt",
    )(_alibi_slopes() * LOG2E, qt16, k16, vt16, pen)


def _matmul_kernel(a_ref, w_ref, o_ref):
    o_ref[...] = _wdot(a_ref[...], w_ref[0])


def _matmul(a, w, layer):
    rows, cols = a.shape[0], w.shape[2]
    return pl.pallas_call(
        _matmul_kernel, grid=(1,),
        in_specs=[pl.BlockSpec(a.shape, lambda g: (0, 0)),
                  pl.BlockSpec((1,) + w.shape[1:], lambda g: (layer, 0, 0))],
        out_specs=pl.BlockSpec((rows, cols), lambda g: (0, 0)),
        out_shape=jax.ShapeDtypeStruct((rows, cols), F32),
        compiler_params=_cparams(("arbitrary",)), name="matmul",
    )(a, w)


def _moba_sample_kernel(pt_ref, q_ref, kn_ref, vn_ref, slope_ref, *refs, past_len, blocks_per_step):
    del pt_ref
    n_pages = 2 * blocks_per_step
    k_pages, v_pages = refs[:n_pages], refs[n_pages:2 * n_pages]
    o_ref, q_lanes, m_s, l_s, g_s, o_s = refs[2 * n_pages:]
    step = pl.program_id(1)
    n_blk = pl.num_programs(1) * blocks_per_step
    page = k_pages[0].shape[-1]
    q_col = q_ref[0]
    blk_lane = lax.broadcasted_iota(jnp.int32, m_s.shape, 2)
    blk_lane_o = lax.broadcasted_iota(jnp.int32, o_s.shape, 2)

    @pl.when(step == 0)
    def _():
        q_lanes[...] = jnp.broadcast_to(q_col, q_lanes.shape)
        m_s[...] = jnp.zeros_like(m_s)
        l_s[...] = jnp.zeros_like(l_s)
        g_s[...] = jnp.zeros_like(g_s)
        o_s[...] = jnp.zeros_like(o_s)

    q_b = q_lanes[...]
    slope = slope_ref[...]
    tok = lax.broadcasted_iota(jnp.int32, (1, 1, page), 2)
    for j in range(blocks_per_step):
        n = step * blocks_per_step + j
        raw = [jnp.sum(k_pages[2 * j + half][0, 0] * q_b, axis=1, keepdims=True) for half in range(2)]
        s_pages = []
        for half, r in enumerate(raw):
            dist = (past_len - (n * MOBA_BLOCK + half * page) - tok).astype(F32)
            s_pages.append(r * (HEAD_DIM ** -0.5) - slope * dist)
        m = jnp.maximum(jnp.max(s_pages[0], axis=-1, keepdims=True),
                        jnp.max(s_pages[1], axis=-1, keepdims=True))
        l = jnp.zeros(m.shape, F32)
        pv = jnp.zeros(q_b.shape, F32)
        for half, s in enumerate(s_pages):
            p = jnp.exp(s - m)
            l = l + jnp.sum(p, axis=-1, keepdims=True)
            pv = pv + p * v_pages[2 * j + half][0, 0]
        gate = (jnp.sum(raw[0], axis=-1, keepdims=True)
                + jnp.sum(raw[1], axis=-1, keepdims=True)) * (1.0 / MOBA_BLOCK)
        m_s[...] = jnp.where(blk_lane == n, m, m_s[...])
        l_s[...] = jnp.where(blk_lane == n, l, l_s[...])
        g_s[...] = jnp.where(blk_lane == n, gate, g_s[...])
        o_s[...] = jnp.where(blk_lane_o == n, jnp.sum(pv, axis=-1, keepdims=True), o_s[...])

    @pl.when(step == pl.num_programs(1) - 1)
    def _():
        gates = jnp.where(blk_lane < n_blk, g_s[...], NEG_INF)
        sel = jnp.zeros(gates.shape, F32)
        for _ in range(MOBA_TOPK):
            g_max = jnp.max(gates, axis=-1, keepdims=True)
            idx = jnp.min(jnp.where(gates == g_max, blk_lane, LANES), axis=-1, keepdims=True)
            hit = blk_lane == idx
            sel = jnp.where(hit & (g_max > NEG_INF), 1.0, sel)
            gates = jnp.where(hit, NEG_INF, gates)
        chosen = sel > 0.0
        s_own = jnp.sum(q_col * kn_ref[0], axis=1, keepdims=True) * (HEAD_DIM ** -0.5)
        m_all = m_s[...]
        m_tot = jnp.maximum(jnp.max(jnp.where(chosen, m_all, NEG_INF), axis=-1, keepdims=True), s_own)
        w = jnp.where(chosen, jnp.exp(m_all - m_tot), 0.0)
        w_own = jnp.exp(s_own - m_tot)
        l_tot = jnp.sum(w * l_s[...], axis=-1, keepdims=True) + w_own
        o_tot = jnp.sum(w * o_s[...], axis=-1, keepdims=True) + w_own * vn_ref[0]
        o_ref[0] = o_tot / l_tot


def _moba_sample(q, k_new, v_new, cache_k, cache_v, page_table, layer):
    dec, n_heads, hd, _ = q.shape
    n_pages = page_table.shape[1]
    page = cache_k.shape[-1]
    assert MOBA_BLOCK == 2 * page
    n_blk = n_pages // 2
    bps = SAMPLE_BLOCKS_PER_STEP if n_blk % SAMPLE_BLOCKS_PER_STEP == 0 else 1
    past_len = n_pages * page
    vec = pl.BlockSpec((1, n_heads, hd, 1), lambda b, n, pt: (b, 0, 0, 0))

    def page_spec(idx):
        return pl.BlockSpec((1, 1, n_heads, hd, page),
                            lambda b, n, pt: (layer, pt[b, 2 * bps * n + idx], 0, 0, 0))

    page_specs = [page_spec(idx) for idx in range(2 * bps)]
    assert n_blk <= LANES
    stat = pltpu.VMEM((n_heads, 1, LANES), F32)
    return pl.pallas_call(
        functools.partial(_moba_sample_kernel, past_len=past_len, blocks_per_step=bps),
        grid_spec=pltpu.PrefetchScalarGridSpec(
            num_scalar_prefetch=1,
            grid=(dec, n_blk // bps),
            in_specs=[vec, vec, vec, pl.BlockSpec((n_heads, 1, 1), lambda b, n, pt: (0, 0, 0))]
            + page_specs + page_specs,
            out_specs=vec,
            scratch_shapes=[pltpu.VMEM((n_heads, hd, page), F32), stat, stat, stat,
                            pltpu.VMEM((n_heads, hd, LANES), F32)],
        ),
        out_shape=jax.ShapeDtypeStruct((dec, n_heads, hd, 1), F32),
        compiler_params=_cparams(("arbitrary", "arbitrary")), name="moba_sample",
    )(page_table, q, k_new, v_new, _alibi_slopes().reshape(n_heads, 1, 1),
      *([cache_k] * (2 * bps)), *([cache_v] * (2 * bps)))


def _trunk(x, mods, p, past):
    bn, t, d = x.shape
    ssm_re, ssm_im, new_k, new_v = [], [], [], []
    depth = len(mods)
    for layer in range(depth):
        i = layer // 2
        sh1, sc1, g1, sh2, sc2, g2 = mods[layer]
        act_dtype = p['attn_w_o'].dtype
        if layer + 1 < depth:
            nxt = (p['norm_mix'][layer + 1], mods[layer + 1][0], mods[layer + 1][1],
                   F32 if layer % 2 else act_dtype)
        else:
            zero = jnp.zeros((bn, 1, d), F32)
            nxt = (p['norm_final'], zero, zero, F32)
        if layer % 2 == 0:
            u = h_mix if layer else _modulate(x, p['norm_mix'][layer], sh1, sc1, F32)
            if past is None:
                g, f_re, f_im = _s5_prompt(u, p['s5_bb16'][i], p['s5_cre16'][i], p['s5_cim16'][i],
                                           p['s5_pow_re'][i], p['s5_pow_im'][i], p['ssm_d'][i])
                ssm_re.append(f_re.reshape(bn, -1, SSM_STATE))
                ssm_im.append(f_im.reshape(bn, -1, SSM_STATE))
            else:
                h_re = past[3][i].reshape(t, -1)
                h_im = past[4][i].reshape(t, -1)
                g, s_re, s_im = _s5_sample(u[0], h_re, h_im, p['s5_bbar_re'][i], p['s5_bbar_im'][i],
                                           p['s5_c_re'][i], p['s5_c_im'][i],
                                           p['s5_pow_re'][i][0:1], p['s5_pow_im'][i][0:1], p['ssm_d'][i])
                g = g[None]
                ssm_re.append(s_re.reshape(t, -1, SSM_STATE))
                ssm_im.append(s_im.reshape(t, -1, SSM_STATE))
            x, h = _proj_residual(x, g, (p['ssm_w_glu_out'], p['ssm_w_glu_gate']), g1, i,
                                  nxt=(p['norm_ffn'][layer], sh2, sc2, act_dtype))
            x, h_mix = _swiglu_residual(x, h, p['ffn_w_gate'][:, None], p['ffn_w_up'][:, None],
                                        p['ffn_w_down'][:, None], g2, i, nxt=nxt)
        else:
            h = h_mix
            if past is None:
                k_t, v_t, k16, qt16, vt16, pen = _qkv_prompt(h, p['attn_w_qkv'], i)
                a = _moba_prompt(qt16, k16, vt16, pen)
                new_k.append(k_t.reshape(bn, N_HEADS, HEAD_DIM, t))
                new_v.append(v_t.reshape(bn, N_HEADS, HEAD_DIM, t))
            else:
                qkv = _matmul(h[0], p['attn_w_qkv'], i)
                q, k, v = (qkv[:, j * d:(j + 1) * d].reshape(t, N_HEADS, HEAD_DIM, 1) for j in range(3))
                a = _moba_sample(q, k, v, past[0], past[1], past[2], i).reshape(1, t, d)
                new_k.append(k.reshape(t, N_HEADS, HEAD_DIM, 1))
                new_v.append(v.reshape(t, N_HEADS, HEAD_DIM, 1))
            x = _proj_residual(x, a, (p['attn_w_o'],), g1, i)
            h, comb = _modulate(x, p['norm_ffn'][layer], sh2, sc2, act_dtype,
                                router=(p['moe_w_router'][i], p['moe_b_router'][i]))
            x, h_mix = _swiglu_residual(x, h, p['moe_w_gate'], p['moe_w_up'], p['moe_w_down'], g2, i,
                                        combine=comb, nxt=nxt)
    y = h_mix
    new_k, new_v = (jnp.stack(z).transpose(0, 1, 4, 2, 3) for z in (new_k, new_v))
    return y, jnp.stack(ssm_re), jnp.stack(ssm_im), new_k, new_v


def kernel(x_prompt, x_sample, cache_k, cache_v, page_table, state_ssm_re, state_ssm_im,
           c_prompt, c_sample, ada_w, ada_b, norm_mix, norm_ffn, norm_final,
           ssm_a_re, ssm_a_im, ssm_log_dt, ssm_b_re, ssm_b_im, ssm_c_re, ssm_c_im, ssm_d,
           ssm_w_glu_out, ssm_w_glu_gate, attn_w_qkv, attn_w_o,
           ffn_w_gate, ffn_w_up, ffn_w_down,
           moe_w_router, moe_b_router, moe_w_gate, moe_w_up, moe_w_down):
    bn, t, d = x_prompt.shape
    dec = x_sample.shape[0]
    depth = ada_w.shape[0]
    assert x_sample.shape[1] == 1 and d == N_HEADS * HEAD_DIM and t % MOBA_BLOCK == 0

    n_cond = bn + dec
    c_all = jnp.pad(jnp.concatenate([c_prompt, c_sample], axis=0), ((0, -n_cond % SUBLANES), (0, 0)))
    mod = _adaln(c_all, ada_w, ada_b)
    mods_p = [[mod[l, :bn, j * d:(j + 1) * d].reshape(bn, 1, d) for j in range(6)] for l in range(depth)]
    mods_s = [[mod[l, bn:n_cond, j * d:(j + 1) * d].reshape(1, dec, d) for j in range(6)] for l in range(depth)]

    pow_re, pow_im, bbar_re, bbar_im = _s5_prep(ssm_a_re, ssm_a_im, ssm_log_dt, ssm_b_re, ssm_b_im,
                                                S5_TILE // SUBLANES)
    c_re_bd, c_im_bd = _blockdiag_c(ssm_c_re), _blockdiag_c(ssm_c_im)
    shared = {
        'norm_mix': norm_mix, 'norm_ffn': norm_ffn, 'norm_final': norm_final, 'ssm_d': ssm_d,
        's5_pow_re': pow_re, 's5_pow_im': pow_im, 's5_bbar_re': bbar_re, 's5_bbar_im': bbar_im,
        's5_bb16': jnp.concatenate([bbar_re, bbar_im], axis=-1).astype(BF16),
        's5_c_re': c_re_bd, 's5_c_im': c_im_bd,
        's5_cre16': c_re_bd.astype(BF16), 's5_cim16': c_im_bd.astype(BF16),
        'moe_w_router': moe_w_router, 'moe_b_router': moe_b_router,
    }
    big = {
        'ssm_w_glu_out': ssm_w_glu_out, 'ssm_w_glu_gate': ssm_w_glu_gate,
        'attn_w_qkv': attn_w_qkv, 'attn_w_o': attn_w_o,
        'ffn_w_gate': ffn_w_gate, 'ffn_w_up': ffn_w_up, 'ffn_w_down': ffn_w_down,
        'moe_w_gate': moe_w_gate, 'moe_w_up': moe_w_up, 'moe_w_down': moe_w_down,
    }
    p_prompt = dict(shared, **{name: w.astype(BF16) for name, w in big.items()})
    p_sample = dict(shared, **big)

    y_p, re_p, im_p, k_p, v_p = _trunk(x_prompt, mods_p, p_prompt, None)

    past = (cache_k.transpose(0, 1, 3, 4, 2), cache_v.transpose(0, 1, 3, 4, 2),
            page_table, state_ssm_re, state_ssm_im)
    y_s, re_s, im_s, k_s, v_s = _trunk(x_sample.reshape(1, dec, d), mods_s, p_sample, past)
    return (y_p, y_s.reshape(dec, 1, d), re_p, im_p, re_s, im_s, k_p, v_p, k_s, v_s)
```
